```python
import math
import jax, jax.numpy as jnp
from jax import lax
import numpy as np

D_MODEL = 2048
BATCH = 4
SEQ = 2048
DEPTH = 4
DEC_BATCH = 8
DEC_SEQ = 4
PAST_LEN = 16384
PAGE_SIZE = 128

N_A_LAYERS = DEPTH // 2
N_B_LAYERS = DEPTH - N_A_LAYERS
GROUP_CHANNELS = 16
N_GROUPS = D_MODEL // GROUP_CHANNELS
STATE_DIM = 64
HEAD_DIM = 128
V_DIM = 2 * HEAD_DIM
N_HEADS = D_MODEL // V_DIM
D_FF = 4 * D_MODEL
Q_BLOCK = 128
ATTN_SCALE = HEAD_DIM ** -0.5
NORM_EPS = 1e-6
DT_MIN = 0.001
DT_MAX = 0.1

kernel_name = 'yoco_s5_diffattn_decoder_step'


def rmsnorm(x, w):
    xf = x.astype(jnp.float32)
    xf = xf * lax.rsqrt(jnp.mean(xf * xf, axis=-1, keepdims=True) + NORM_EPS)
    return (xf * w.astype(jnp.float32)).astype(x.dtype)


def modulate(h, shift, scale):
    return h * (1 + scale[:, None, :]) + shift[:, None, :]


def cmul(ar, ai, br, bi):
    return ar * br - ai * bi, ar * bi + ai * br


def sq_relu_mlp(h, w_up, w_down):
    return jnp.square(jax.nn.relu(h @ w_up)) @ w_down


def s5_mixer(u, s0_re, s0_im, lam_re, lam_im, log_step, b_re, b_im, c_re, c_im, d_skip, glu_w):
    bsz, t = u.shape[:2]
    f32 = jnp.float32
    lr = lam_re.astype(f32)
    li = lam_im.astype(f32)
    dt = jnp.exp(log_step.astype(f32))[:, None]
    mag = jnp.exp(lr * dt)
    abar_re = mag * jnp.cos(li * dt)
    abar_im = mag * jnp.sin(li * dt)
    den = lr * lr + li * li
    xr = abar_re - 1.0
    f_re = (xr * lr + abar_im * li) / den
    f_im = (abar_im * lr - xr * li) / den
    bb_re, bb_im = cmul(f_re[..., None], f_im[..., None], b_re.astype(f32), b_im.astype(f32))
    ug = u.astype(f32).reshape(bsz, t, N_GROUPS, GROUP_CHANNELS)
    bu_re = jnp.einsum('btgc,gnc->tbgn', ug, bb_re)
    bu_im = jnp.einsum('btgc,gnc->tbgn', ug, bb_im)
    i_re, i_im = cmul(abar_re, abar_im, s0_re.astype(f32), s0_im.astype(f32))
    bu_re = bu_re.at[0].add(i_re)
    bu_im = bu_im.at[0].add(i_im)
    a_re = jnp.broadcast_to(abar_re, (t, 1, N_GROUPS, STATE_DIM))
    a_im = jnp.broadcast_to(abar_im, (t, 1, N_GROUPS, STATE_DIM))

    def combine(e1, e2):
        a1r, a1i, b1r, b1i = e1
        a2r, a2i, b2r, b2i = e2
        ar, ai = cmul(a2r, a2i, a1r, a1i)
        br, bi = cmul(a2r, a2i, b1r, b1i)
        return ar, ai, br + b2r, bi + b2i

    _, _, s_re, s_im = lax.associative_scan(combine, (a_re, a_im, bu_re, bu_im), axis=0)
    y = jnp.einsum('tbgn,gcn->btgc', s_re, c_re.astype(f32)) - jnp.einsum('tbgn,gcn->btgc', s_im, c_im.astype(f32))
    y = y.reshape(bsz, t, D_MODEL) + d_skip.astype(f32) * u.astype(f32)
    h = jax.nn.gelu(y).astype(u.dtype)
    val, gate = jnp.split(h @ glu_w, 2, axis=-1)
    return val * jax.nn.sigmoid(gate), s_re[-1], s_im[-1]


def shared_kv(x, c, kv_ada_w, kv_ada_b, kv_norm_w, kv_w, k_norm_w):
    bsz, t = x.shape[:2]
    shift, scale = jnp.split(c @ kv_ada_w + kv_ada_b, 2, axis=-1)
    h = modulate(rmsnorm(x, kv_norm_w), shift, scale)
    kv = h @ kv_w
    k, v = jnp.split(kv, [N_HEADS * 2 * HEAD_DIM], axis=-1)
    k = rmsnorm(k.reshape(bsz, t, N_HEADS, 2, HEAD_DIM), k_norm_w)
    return k.reshape(bsz, t, N_HEADS, 2 * HEAD_DIM), v.reshape(bsz, t, N_HEADS, V_DIM)


def diff_attend_core(q, ks, vs, masks, lam):
    scores = []
    for k, m in zip(ks, masks):
        s = jnp.einsum('bqhmd,bkhmd->bmhqk', q, k, preferred_element_type=jnp.float32) * ATTN_SCALE
        if m is not None:
            s = jnp.where(m, s, -jnp.inf)
        scores.append(s)
    p = jax.nn.softmax(jnp.concatenate(scores, axis=-1), axis=-1)
    a = p[:, 0] - lam * p[:, 1]
    out = 0
    start = 0
    for v in vs:
        n = v.shape[1]
        out = out + jnp.einsum('bhqk,bkhv->bqhv', a[..., start:start + n].astype(v.dtype), v,
                               preferred_element_type=jnp.float32)
        start += n
    return out.astype(q.dtype)


def prompt_attend(q, k, v, lam):
    bsz, t = q.shape[:2]
    nb = t // Q_BLOCK
    k5 = k.reshape(bsz, t, N_HEADS, 2, HEAD_DIM)
    qb = jnp.moveaxis(q.reshape(bsz, nb, Q_BLOCK, N_HEADS, 2, HEAD_DIM), 1, 0)
    kpos = jnp.arange(t)

    def one_block(args):
        q_blk, blk = args
        qpos = blk * Q_BLOCK + jnp.arange(Q_BLOCK)
        mask = kpos[None, :] <= qpos[:, None]
        return diff_attend_core(q_blk, [k5], [v], [mask], lam)

    out = lax.map(one_block, (qb, jnp.arange(nb)))
    return jnp.moveaxis(out, 0, 1).reshape(bsz, t, N_HEADS, V_DIM)


def make_sample_attend(k_past, v_past):
    def attend(q, k, v, lam):
        bsz, n = q.shape[:2]
        causal = jnp.arange(n)[None, :] <= jnp.arange(n)[:, None]
        k5 = k.reshape(bsz, n, N_HEADS, 2, HEAD_DIM)
        return diff_attend_core(q, [k_past, k5], [v_past, v], [None, causal], lam)
    return attend


def diff_attn_mixer(h, k, v, attend, wq, q_norm_w, lq1, lk1, lq2, lk2, subln_w, wo, lam_init):
    bsz, t = h.shape[:2]
    f32 = jnp.float32
    q = rmsnorm((h @ wq).reshape(bsz, t, N_HEADS, 2, HEAD_DIM), q_norm_w)
    lam = (jnp.exp(jnp.sum(lq1.astype(f32) * lk1.astype(f32)))
           - jnp.exp(jnp.sum(lq2.astype(f32) * lk2.astype(f32))) + lam_init)
    o = attend(q, k, v, lam)
    o = rmsnorm(o, subln_w) * (1.0 - lam_init)
    return o.reshape(bsz, t, D_MODEL) @ wo


def run_group(x, c, s_re0, s_im0, attend, p):
    new_re, new_im = [], []
    k = v = None
    for l in range(DEPTH):
        mod = c @ p['ada_w'][l] + p['ada_b'][l]
        sh1, sc1, g1, sh2, sc2, g2 = jnp.split(mod, 6, axis=-1)
        h = modulate(rmsnorm(x, p['norm_w'][l, 0]), sh1, sc1)
        if l < N_A_LAYERS:
            y, sr, si = s5_mixer(h, s_re0[l], s_im0[l], p['ssm_lambda_re'][l], p['ssm_lambda_im'][l],
                                 p['ssm_log_step'][l], p['ssm_b_re'][l], p['ssm_b_im'][l],
                                 p['ssm_c_re'][l], p['ssm_c_im'][l], p['ssm_d'][l], p['glu_w'][l])
            new_re.append(sr)
            new_im.append(si)
        else:
            j = l - N_A_LAYERS
            lam_init = 0.8 - 0.6 * math.exp(-0.3 * l)
            y = diff_attn_mixer(h, k, v, attend, p['attn_wq'][j], p['q_norm_w'][j], p['lambda_q1'][j],
                                p['lambda_k1'][j], p['lambda_q2'][j], p['lambda_k2'][j], p['subln_w'][j],
                                p['attn_wo'][j], lam_init)
        x = x + g1[:, None, :] * y
        h = modulate(rmsnorm(x, p['norm_w'][l, 1]), sh2, sc2)
        x = x + g2[:, None, :] * sq_relu_mlp(h, p['mlp_up'][l], p['mlp_down'][l])
        if l == N_A_LAYERS - 1:
            k, v = shared_kv(x, c, p['kv_ada_w'], p['kv_ada_b'], p['kv_norm_w'], p['kv_w'], p['k_norm_w'])
    return x, jnp.stack(new_re), jnp.stack(new_im), k, v


def setup_inputs(seed: int = 0) -> dict:
    key = jax.random.key(seed)
    ks = jax.random.split(key, 40)
    f32 = jnp.float32
    n_pages = PAST_LEN // PAGE_SIZE
    n_used = DEC_BATCH * n_pages
    n_pool = (5 * n_used) // 4
    inv = D_MODEL ** -0.5

    def nrm(k, shape, s):
        return jax.random.normal(k, shape, f32) * s

    page_table = jax.random.permutation(ks[6], n_pool)[:n_used].reshape(DEC_BATCH, n_pages).astype(jnp.int32)
    log_step = (math.log(DT_MIN) + jax.random.uniform(ks[12], (N_A_LAYERS, N_GROUPS), f32)
                * (math.log(DT_MAX) - math.log(DT_MIN)))
    lam_im = math.pi * jnp.arange(STATE_DIM, dtype=f32) + nrm(ks[11], (N_A_LAYERS, N_GROUPS, STATE_DIM), 0.01)
    return {
        'x_prompt': nrm(ks[0], (BATCH, SEQ, D_MODEL), 1.0),
        'x_sample': nrm(ks[1], (DEC_BATCH, DEC_SEQ, D_MODEL), 1.0),
        'state_ssm_re': nrm(ks[2], (N_A_LAYERS, DEC_BATCH, N_GROUPS, STATE_DIM), 0.3),
        'state_ssm_im': nrm(ks[3], (N_A_LAYERS, DEC_BATCH, N_GROUPS, STATE_DIM), 0.3),
        'cache_k': nrm(ks[4], (n_pool, PAGE_SIZE, N_HEADS, 2 * HEAD_DIM), 1.0),
        'cache_v': nrm(ks[5], (n_pool, PAGE_SIZE, N_HEADS, V_DIM), 1.0),
        'page_table': page_table,
        'c_prompt': nrm(ks[7], (BATCH, D_MODEL), 1.0),
        'c_sample': nrm(ks[8], (DEC_BATCH, D_MODEL), 1.0),
        'ada_w': nrm(ks[9], (DEPTH, D_MODEL, 6 * D_MODEL), 0.5 * inv),
        'ada_b': nrm(ks[10], (DEPTH, 6 * D_MODEL), 0.01),
        'norm_w': 1.0 + nrm(ks[13], (DEPTH, 2, D_MODEL), 0.01),
        'mlp_up': nrm(ks[14], (DEPTH, D_MODEL, D_FF), inv),
        'mlp_down': nrm(ks[15], (DEPTH, D_FF, D_MODEL), D_FF ** -0.5),
        'ssm_lambda_re': -0.5 + nrm(ks[16], (N_A_LAYERS, N_GROUPS, STATE_DIM), 0.01),
        'ssm_lambda_im': lam_im,
        'ssm_log_step': log_step,
        'ssm_b_re': nrm(ks[17], (N_A_LAYERS, N_GROUPS, STATE_DIM, GROUP_CHANNELS), (2 * GROUP_CHANNELS) ** -0.5),
        'ssm_b_im': nrm(ks[18], (N_A_LAYERS, N_GROUPS, STATE_DIM, GROUP_CHANNELS), (2 * GROUP_CHANNELS) ** -0.5),
        'ssm_c_re': nrm(ks[19], (N_A_LAYERS, N_GROUPS, GROUP_CHANNELS, STATE_DIM), STATE_DIM ** -0.5),
        'ssm_c_im': nrm(ks[20], (N_A_LAYERS, N_GROUPS, GROUP_CHANNELS, STATE_DIM), STATE_DIM ** -0.5),
        'ssm_d': nrm(ks[21], (N_A_LAYERS, D_MODEL), 1.0),
        'glu_w': nrm(ks[22], (N_A_LAYERS, D_MODEL, 2 * D_MODEL), inv),
        'kv_ada_w': nrm(ks[23], (D_MODEL, 2 * D_MODEL), 0.5 * inv),
        'kv_ada_b': nrm(ks[24], (2 * D_MODEL,), 0.01),
        'kv_norm_w': 1.0 + nrm(ks[25], (D_MODEL,), 0.01),
        'kv_w': nrm(ks[26], (D_MODEL, N_HEADS * (2 * HEAD_DIM + V_DIM)), inv),
        'k_norm_w': 1.0 + nrm(ks[27], (HEAD_DIM,), 0.01),
        'attn_wq': nrm(ks[28], (N_B_LAYERS, D_MODEL, N_HEADS * 2 * HEAD_DIM), inv),
        'q_norm_w': 1.0 + nrm(ks[29], (N_B_LAYERS, HEAD_DIM), 0.01),
        'lambda_q1': nrm(ks[30], (N_B_LAYERS, HEAD_DIM), 0.1),
        'lambda_k1': nrm(ks[31], (N_B_LAYERS, HEAD_DIM), 0.1),
        'lambda_q2': nrm(ks[32], (N_B_LAYERS, HEAD_DIM), 0.1),
        'lambda_k2': nrm(ks[33], (N_B_LAYERS, HEAD_DIM), 0.1),
        'subln_w': 1.0 + nrm(ks[34], (N_B_LAYERS, V_DIM), 0.01),
        'attn_wo': nrm(ks[35], (N_B_LAYERS, D_MODEL, D_MODEL), inv),
    }


def reference(x_prompt, x_sample, state_ssm_re, state_ssm_im, cache_k, cache_v, page_table, c_prompt, c_sample,
              ada_w, ada_b, norm_w, mlp_up, mlp_down, ssm_lambda_re, ssm_lambda_im, ssm_log_step,
              ssm_b_re, ssm_b_im, ssm_c_re, ssm_c_im, ssm_d, glu_w, kv_ada_w, kv_ada_b, kv_norm_w, kv_w,
              k_norm_w, attn_wq, q_norm_w, lambda_q1, lambda_k1, lambda_q2, lambda_k2, subln_w, attn_wo):
    p = {
        'ada_w': ada_w, 'ada_b': ada_b, 'norm_w': norm_w, 'mlp_up': mlp_up, 'mlp_down': mlp_down,
        'ssm_lambda_re': ssm_lambda_re, 'ssm_lambda_im': ssm_lambda_im, 'ssm_log_step': ssm_log_step,
        'ssm_b_re': ssm_b_re, 'ssm_b_im': ssm_b_im, 'ssm_c_re': ssm_c_re, 'ssm_c_im': ssm_c_im,
        'ssm_d': ssm_d, 'glu_w': glu_w, 'kv_ada_w': kv_ada_w, 'kv_ada_b': kv_ada_b, 'kv_norm_w': kv_norm_w,
        'kv_w': kv_w, 'k_norm_w': k_norm_w, 'attn_wq': attn_wq, 'q_norm_w': q_norm_w,
        'lambda_q1': lambda_q1, 'lambda_k1': lambda_k1, 'lambda_q2': lambda_q2, 'lambda_k2': lambda_k2,
        'subln_w': subln_w, 'attn_wo': attn_wo,
    }
    zeros = jnp.zeros((N_A_LAYERS, x_prompt.shape[0], N_GROUPS, STATE_DIM), jnp.float32)
    y_prompt, ssm_re_prompt, ssm_im_prompt, k_prompt, v_prompt = run_group(
        x_prompt, c_prompt, zeros, zeros, prompt_attend, p)
    n_pages = PAST_LEN // PAGE_SIZE
    db = x_sample.shape[0]
    k_past = cache_k[page_table].reshape(db, n_pages * PAGE_SIZE, N_HEADS, 2, HEAD_DIM)
    v_past = cache_v[page_table].reshape(db, n_pages * PAGE_SIZE, N_HEADS, V_DIM)
    y_sample, ssm_re_sample, ssm_im_sample, k_sample, v_sample = run_group(
        x_sample, c_sample, state_ssm_re, state_ssm_im, make_sample_attend(k_past, v_past), p)
    return (y_prompt, y_sample, ssm_re_prompt, ssm_im_prompt, k_prompt, v_prompt,
            ssm_re_sample, ssm_im_sample, k_sample, v_sample)
```

```python
import functools
import math

import jax
import jax.numpy as jnp
from jax import lax
from jax.experimental import pallas as pl
from jax.experimental.pallas import tpu as pltpu

F32 = jnp.float32
BF16 = jnp.bfloat16
HI = lax.Precision.HIGHEST

NORM_EPS = 1e-6
SSM_CHUNK = 16
MIB = 1024 * 1024
VMEM_LIMIT_BYTES = 52 * MIB
ATTN_TILE = 512


def _params(*sem):
    return pltpu.CompilerParams(dimension_semantics=sem, vmem_limit_bytes=VMEM_LIMIT_BYTES)


def _tile(dim, pref):
    t = min(dim, pref)
    assert dim % t == 0, (dim, pref)
    return t


def _dot(a, b):
    return jnp.dot(a, b, preferred_element_type=F32)


def _dot_nt(a, b):
    return lax.dot_general(a, b, (((1,), (1,)), ((), ())), preferred_element_type=F32)


def _rms(x, w):
    return x * lax.rsqrt(jnp.mean(x * x, axis=-1, keepdims=True) + NORM_EPS) * w


def _norm_mod(x, nw, sh, sc):
    return _rms(x, nw) * (1.0 + sc) + sh


def _chunk_rms(y, w, width):
    parts = []
    for c in range(y.shape[-1] // width):
        parts.append(_rms(y[:, c * width:(c + 1) * width], w))
    return parts[0] if len(parts) == 1 else jnp.concatenate(parts, axis=-1)


def _ada_kernel(c_ref, w_ref, b_ref, o_ref):
    o_ref[0] = _dot(c_ref[...], w_ref[0].astype(BF16)) + b_ref[0]


def _ada(c, w, b):
    nl, d, n = w.shape
    r = c.shape[0]
    tn = _tile(n, 512)
    return pl.pallas_call(
        _ada_kernel,
        grid=(nl, n // tn),
        in_specs=[pl.BlockSpec((r, d), lambda l, j: (0, 0)),
                  pl.BlockSpec((1, d, tn), lambda l, j: (l, 0, j)),
                  pl.BlockSpec((1, 1, tn), lambda l, j: (l, 0, j))],
        out_specs=pl.BlockSpec((1, r, tn), lambda l, j: (l, 0, j)),
        out_shape=jax.ShapeDtypeStruct((nl, r, n), F32),
        compiler_params=_params("parallel", "parallel"),
        name="ada",
    )(c, w, b)


class _Rows:
    def __init__(self, m, nb, tm_pref):
        self.m = m
        self.nb = nb
        rows_per_nb = m // nb
        self.tm = _tile(rows_per_nb, tm_pref)
        self.tiles_per_nb = rows_per_nb // self.tm
        self.nt = m // self.tm

    def mod_spec(self, r, width, col=None):
        tpn = self.tiles_per_nb
        if col is None:
            return pl.BlockSpec((1, r, width), lambda i, j: (i // tpn, 0, 0))
        return pl.BlockSpec((1, r, width), lambda i, j: (i // tpn, 0, j))


def _norm_mod_kernel(x_ref, nw_ref, sh_ref, sc_ref, o_ref):
    o_ref[...] = _norm_mod(x_ref[...], nw_ref[...], sh_ref[0], sc_ref[0]).astype(BF16)


def _norm_mod_call(x, nw, sh, sc, nb):
    m, d = x.shape
    rows = _Rows(m, nb, 512)
    tm, r = rows.tm, sh.shape[1]
    tpn = rows.tiles_per_nb
    return pl.pallas_call(
        _norm_mod_kernel,
        grid=(rows.nt,),
        in_specs=[pl.BlockSpec((tm, d), lambda i: (i, 0)),
                  pl.BlockSpec((1, d), lambda i: (0, 0)),
                  pl.BlockSpec((1, r, d), lambda i: (i // tpn, 0, 0)),
                  pl.BlockSpec((1, r, d), lambda i: (i // tpn, 0, 0))],
        out_specs=pl.BlockSpec((tm, d), lambda i: (i, 0)),
        out_shape=jax.ShapeDtypeStruct((m, d), BF16),
        compiler_params=_params("parallel"),
        name="norm_mod",
    )(x, nw, sh, sc)


def _cmul(a_ref_val, s, half):
    return a_ref_val[0:1, :] * s + a_ref_val[1:2, :] * pltpu.roll(s, half, axis=1)


def _ssm_kernel(x_ref, toep_ref, wst_ref, wout_ref, apow_ref, s0_ref, y_ref, sf_ref, *, gb, nb, nc, half):
    r = nb * nc
    nsteps = nc.bit_length() - 1
    row = lax.broadcasted_iota(jnp.int32, (r, 2 * half), 0)
    chunk = row & (nc - 1)

    def body(gi, carry):
        x = x_ref[gi]
        y = _dot(x, toep_ref[gi])
        p = _dot(x, wst_ref[gi])
        s0 = s0_ref[gi]
        s0_rows = jnp.zeros((r, 2 * half), F32)
        for b in range(nb):
            s0_rows = jnp.where(row == b * nc, s0[b:b + 1, :], s0_rows)
        p = p + _cmul(apow_ref[gi, 0], s0_rows, half)
        for k in range(nsteps):
            d = 1 << k
            shifted = jnp.where(chunk >= d, pltpu.roll(p, d, axis=0), 0.0)
            p = p + _cmul(apow_ref[gi, k], shifted, half)
        s_in = s0_rows if nc == 1 else jnp.where(chunk == 0, s0_rows, pltpu.roll(p, 1, axis=0))
        y_ref[gi] = y + _dot(s_in.astype(BF16), wout_ref[gi])
        for b in range(nb):
            last = b * nc + nc - 1
            sf_ref[gi, b:b + 1, :] = p[last:last + 1, :]
        return carry

    lax.fori_loop(0, gb, body, 0)


def _ssm_call(xg, toep, wst, wout, apow, s0, nb):
    g, r, w = xg.shape
    nc = r // nb
    assert nc & (nc - 1) == 0
    n2 = wst.shape[2]
    ns = apow.shape[1]
    gb = _tile(g, 8)
    kern = functools.partial(_ssm_kernel, gb=gb, nb=nb, nc=nc, half=n2 // 2)
    return pl.pallas_call(
        kern,
        grid=(g // gb,),
        in_specs=[pl.BlockSpec((gb, r, w), lambda i: (i, 0, 0)),
                  pl.BlockSpec((gb, w, w), lambda i: (i, 0, 0)),
                  pl.BlockSpec((gb, w, n2), lambda i: (i, 0, 0)),
                  pl.BlockSpec((gb, n2, w), lambda i: (i, 0, 0)),
                  pl.BlockSpec((gb, ns, 2, n2), lambda i: (i, 0, 0, 0)),
                  pl.BlockSpec((gb, nb, n2), lambda i: (i, 0, 0))],
        out_specs=[pl.BlockSpec((gb, r, w), lambda i: (i, 0, 0)),
                   pl.BlockSpec((gb, nb, n2), lambda i: (i, 0, 0))],
        out_shape=[jax.ShapeDtypeStruct((g, r, w), F32),
                   jax.ShapeDtypeStruct((g, nb, n2), F32)],
        compiler_params=_params("parallel"),
        name="ssm",
    )(xg, toep, wst, wout, apow, s0)


def _ssm_tables(lam_re, lam_im, log_step, b_re, b_im, c_re, c_im, tv, nc):
    L = SSM_CHUNK
    g, n = lam_re.shape
    gc = b_re.shape[2]
    lr, li = lam_re.astype(F32), lam_im.astype(F32)
    dt = jnp.exp(log_step.astype(F32))[:, None]
    mag = jnp.exp(lr * dt)
    ar, ai = mag * jnp.cos(li * dt), mag * jnp.sin(li * dt)
    den = lr * lr + li * li
    xr = ar - 1.0
    f_re = (xr * lr + ai * li) / den
    f_im = (ai * lr - xr * li) / den
    bb_re = f_re[..., None] * b_re - f_im[..., None] * b_im
    bb_im = f_re[..., None] * b_im + f_im[..., None] * b_re
    j = jnp.arange(L + 1, dtype=F32)[:, None, None]
    pmag = jnp.exp(j * (lr * dt))
    pw_re, pw_im = pmag * jnp.cos(j * (li * dt)), pmag * jnp.sin(j * (li * dt))
    pb_re = pw_re[..., None] * bb_re - pw_im[..., None] * bb_im
    pb_im = pw_re[..., None] * bb_im + pw_im[..., None] * bb_re
    kj = (jnp.einsum('gcn,jgnd->jgcd', c_re, pb_re[:L], precision=HI)
          - jnp.einsum('gcn,jgnd->jgcd', c_im, pb_im[:L], precision=HI))
    tk = jnp.arange(L)
    lag = tk[None, :] - tk[:, None]
    t5 = jnp.where((lag >= 0)[:, :, None, None, None], kj[jnp.clip(lag, 0, L - 1)], 0.0)
    toep = jnp.transpose(t5, (2, 0, 4, 1, 3)).reshape(g, L * gc, L * gc)
    sel = jnp.clip(tv - 1 - tk, 0, L)
    valid = (tk < tv)[:, None, None, None]
    ws_re = jnp.where(valid, pb_re[sel], 0.0)
    ws_im = jnp.where(valid, pb_im[sel], 0.0)
    wst = jnp.concatenate([jnp.transpose(ws_re, (1, 0, 3, 2)), jnp.transpose(ws_im, (1, 0, 3, 2))], axis=-1)
    wst = wst.reshape(g, L * gc, 2 * n)
    cp_re = c_re[None] * pw_re[1:, :, None, :] - c_im[None] * pw_im[1:, :, None, :]
    cp_im = c_re[None] * pw_im[1:, :, None, :] + c_im[None] * pw_re[1:, :, None, :]
    wout = jnp.concatenate([jnp.transpose(cp_re, (1, 3, 0, 2)), -jnp.transpose(cp_im, (1, 3, 0, 2))], axis=1)
    wout = wout.reshape(g, 2 * n, L * gc)
    are, aim = pw_re[tv], pw_im[tv]
    steps = []
    for _ in range(max(nc.bit_length() - 1, 1)):
        steps.append(jnp.stack([jnp.concatenate([are, are], -1), jnp.concatenate([-aim, aim], -1)], axis=1))
        are, aim = are * are - aim * aim, 2.0 * are * aim
    apow = jnp.stack(steps, axis=1)
    return toep.astype(BF16), wst.astype(BF16), wout.astype(BF16), apow


def _glu_kernel(x_ref, y_ref, nw_ref, sh_ref, sc_ref, d_ref, wv_ref, wg_ref, xr_ref, g_ref, o_ref, a_scr):
    @pl.when(pl.program_id(1) == 0)
    def _():
        h = _norm_mod(x_ref[...], nw_ref[...], sh_ref[0], sc_ref[0])
        a_scr[...] = jax.nn.gelu(y_ref[...] + d_ref[...] * h).astype(BF16)

    a = a_scr[...]
    val = _dot(a, wv_ref[...])
    gate = _dot(a, wg_ref[...])
    o_ref[...] = xr_ref[...] + g_ref[0] * (val / (1.0 + jnp.exp(-gate)))


def _glu_call(x, y, nw, sh, sc, dskip, w, g1, nb):
    m, d = x.shape
    rows = _Rows(m, nb, 512)
    tm, r = rows.tm, sh.shape[1]
    tn = _tile(d, 512)
    nn = d // tn
    return pl.pallas_call(
        _glu_kernel,
        grid=(rows.nt, nn),
        in_specs=[pl.BlockSpec((tm, d), lambda i, j: (i, 0)),
                  pl.BlockSpec((tm, d), lambda i, j: (i, 0)),
                  pl.BlockSpec((1, d), lambda i, j: (0, 0)),
                  rows.mod_spec(r, d), rows.mod_spec(r, d),
                  pl.BlockSpec((1, d), lambda i, j: (0, 0)),
                  pl.BlockSpec((d, tn), lambda i, j: (0, j)),
                  pl.BlockSpec((d, tn), lambda i, j: (0, j + nn)),
                  pl.BlockSpec((tm, tn), lambda i, j: (i, j)),
                  rows.mod_spec(r, tn, col=True)],
        out_specs=pl.BlockSpec((tm, tn), lambda i, j: (i, j)),
        out_shape=jax.ShapeDtypeStruct((m, d), F32),
        scratch_shapes=[pltpu.VMEM((tm, d), BF16)],
        compiler_params=_params("parallel", "arbitrary"),
        name="glu",
    )(x, y, nw, sh, sc, dskip, w, w, x, g1)


def _mlp_kernel(x_ref, nw_ref, sh_ref, sc_ref, g_ref, up_ref, dn_ref, o_ref, h_scr, *, nf):
    f = pl.program_id(1)

    @pl.when(f == 0)
    def _():
        h_scr[...] = _norm_mod(x_ref[...], nw_ref[...], sh_ref[0], sc_ref[0]).astype(BF16)

    a = jnp.maximum(_dot(h_scr[...], up_ref[...]), 0.0)
    part = _dot((a * a).astype(BF16), dn_ref[...])

    @pl.when(f == 0)
    def _():
        o_ref[...] = part

    @pl.when(f > 0)
    def _():
        o_ref[...] += part

    @pl.when(f == nf - 1)
    def _():
        o_ref[...] = x_ref[...] + g_ref[0] * o_ref[...]


def _mlp_call(x, nw, sh, sc, g2, up, dn, nb):
    m, d = x.shape
    ff = up.shape[1]
    rows = _Rows(m, nb, 512)
    tm, r = rows.tm, sh.shape[1]
    tf = _tile(ff, 1024)
    nf = ff // tf
    return pl.pallas_call(
        functools.partial(_mlp_kernel, nf=nf),
        grid=(rows.nt, nf),
        in_specs=[pl.BlockSpec((tm, d), lambda i, j: (i, 0)),
                  pl.BlockSpec((1, d), lambda i, j: (0, 0)),
                  rows.mod_spec(r, d), rows.mod_spec(r, d), rows.mod_spec(r, d),
                  pl.BlockSpec((d, tf), lambda i, j: (0, j)),
                  pl.BlockSpec((tf, d), lambda i, j: (j, 0))],
        out_specs=pl.BlockSpec((tm, d), lambda i, j: (i, 0)),
        out_shape=jax.ShapeDtypeStruct((m, d), F32),
        scratch_shapes=[pltpu.VMEM((tm, d), BF16)],
        compiler_params=_params("parallel", "arbitrary"),
        name="mlp",
    )(x, nw, sh, sc, g2, up, dn)


def _qproj_kernel(x_ref, nw_ref, sh_ref, sc_ref, w_ref, cw_ref, q_ref, h_scr, *, hd):
    @pl.when(pl.program_id(1) == 0)
    def _():
        h_scr[...] = _norm_mod(x_ref[...], nw_ref[...], sh_ref[0], sc_ref[0]).astype(BF16)

    q_ref[...] = _chunk_rms(_dot(h_scr[...], w_ref[...]), cw_ref[...], hd).astype(BF16)


def _qproj_call(x, nw, sh, sc, wq, qnw, nb):
    m, d = x.shape
    n = wq.shape[1]
    hd = qnw.shape[1]
    rows = _Rows(m, nb, 512)
    tm, r = rows.tm, sh.shape[1]
    tn = _tile(n, 512)
    return pl.pallas_call(
        functools.partial(_qproj_kernel, hd=hd),
        grid=(rows.nt, n // tn),
        in_specs=[pl.BlockSpec((tm, d), lambda i, j: (i, 0)),
                  pl.BlockSpec((1, d), lambda i, j: (0, 0)),
                  rows.mod_spec(r, d), rows.mod_spec(r, d),
                  pl.BlockSpec((d, tn), lambda i, j: (0, j)),
                  pl.BlockSpec((1, hd), lambda i, j: (0, 0))],
        out_specs=pl.BlockSpec((tm, tn), lambda i, j: (i, j)),
        out_shape=jax.ShapeDtypeStruct((m, n), BF16),
        scratch_shapes=[pltpu.VMEM((tm, d), BF16)],
        compiler_params=_params("parallel", "arbitrary"),
        name="q_proj",
    )(x, nw, sh, sc, wq, qnw)


def _kvproj_kernel(x_ref, nw_ref, sh_ref, sc_ref, wk_ref, wv_ref, cw_ref, k_ref, v_ref, kb_ref, vb_ref, h_scr,
                   *, hd):
    @pl.when(pl.program_id(1) == 0)
    def _():
        h_scr[...] = _norm_mod(x_ref[...], nw_ref[...], sh_ref[0], sc_ref[0]).astype(BF16)

    h = h_scr[...]
    k = _chunk_rms(_dot(h, wk_ref[...]), cw_ref[...], hd)
    v = _dot(h, wv_ref[...])
    k_ref[...] = k
    v_ref[...] = v
    kb_ref[...] = k.astype(BF16)
    vb_ref[...] = v.astype(BF16)


def _kvproj_call(x, nw, sh, sc, w, knw, nb):
    m, d = x.shape
    nk = w.shape[1] // 2
    hd = knw.shape[1]
    rows = _Rows(m, nb, 512)
    tm, r = rows.tm, sh.shape[1]
    tn = _tile(nk, 512)
    nn = nk // tn
    out_spec = pl.BlockSpec((tm, tn), lambda i, j: (i, j))
    return pl.pallas_call(
        functools.partial(_kvproj_kernel, hd=hd),
        grid=(rows.nt, nn),
        in_specs=[pl.BlockSpec((tm, d), lambda i, j: (i, 0)),
                  pl.BlockSpec((1, d), lambda i, j: (0, 0)),
                  rows.mod_spec(r, d), rows.mod_spec(r, d),
                  pl.BlockSpec((d, tn), lambda i, j: (0, j)),
                  pl.BlockSpec((d, tn), lambda i, j: (0, j + nn)),
                  pl.BlockSpec((1, hd), lambda i, j: (0, 0))],
        out_specs=[out_spec, out_spec, out_spec, out_spec],
        out_shape=[jax.ShapeDtypeStruct((m, nk), F32), jax.ShapeDtypeStruct((m, nk), F32),
                   jax.ShapeDtypeStruct((m, nk), BF16), jax.ShapeDtypeStruct((m, nk), BF16)],
        scratch_shapes=[pltpu.VMEM((tm, d), BF16)],
        compiler_params=_params("parallel", "arbitrary"),
        name="kv_proj",
    )(x, nw, sh, sc, w, w, knw)


def _oproj_kernel(o_ref, w_ref, x_ref, g_ref, y_ref):
    y_ref[...] = x_ref[...] + g_ref[0] * _dot(o_ref[...], w_ref[...])


def _oproj_call(o, wo, x, g1, nb):
    m, d = x.shape
    rows = _Rows(m, nb, 512)
    tm, r = rows.tm, g1.shape[1]
    tn = _tile(d, 512)
    return pl.pallas_call(
        _oproj_kernel,
        grid=(rows.nt, d // tn),
        in_specs=[pl.BlockSpec((tm, o.shape[1]), lambda i, j: (i, 0)),
                  pl.BlockSpec((o.shape[1], tn), lambda i, j: (0, j)),
                  pl.BlockSpec((tm, tn), lambda i, j: (i, j)),
                  rows.mod_spec(r, tn, col=True)],
        out_specs=pl.BlockSpec((tm, tn), lambda i, j: (i, j)),
        out_shape=jax.ShapeDtypeStruct((m, d), F32),
        compiler_params=_params("parallel", "parallel"),
        name="o_proj",
    )(o, wo, x, g1)


def _lambda(lq1, lk1, lq2, lk2, lam_init):
    return (jnp.exp(jnp.sum(lq1 * lk1, axis=-1, keepdims=True))
            - jnp.exp(jnp.sum(lq2 * lk2, axis=-1, keepdims=True)) + lam_init)


def _softmax_step(s, v, m_prev, l_prev, acc_prev):
    m_new = jnp.maximum(m_prev, jnp.max(s, axis=-1, keepdims=True))
    alpha = jnp.exp(m_prev - m_new)
    p = jnp.exp(s - m_new)
    l_new = alpha * l_prev + jnp.sum(p, axis=-1, keepdims=True)
    acc_new = alpha * acc_prev + _dot(p.astype(BF16), v)
    return m_new, l_new, acc_new


def _pattn_kernel(q_ref, k_ref, v_ref, lq1_ref, lk1_ref, lq2_ref, lk2_ref, sw_ref, o_ref, m_scr, l_scr, acc_scr,
                  *, hd, tq, scale, lam_init):
    qi = pl.program_id(2)
    ki = pl.program_id(3)

    @pl.when(ki == 0)
    def _():
        m_scr[...] = jnp.full(m_scr.shape, -jnp.inf, F32)
        l_scr[...] = jnp.zeros(l_scr.shape, F32)
        acc_scr[...] = jnp.zeros(acc_scr.shape, F32)

    @pl.when(ki <= qi)
    def _():
        rowi = qi * tq + lax.broadcasted_iota(jnp.int32, (tq, tq), 0)
        coli = ki * tq + lax.broadcasted_iota(jnp.int32, (tq, tq), 1)
        keep = coli <= rowi
        v = v_ref[...]
        for mp in range(2):
            s = _dot_nt(q_ref[:, mp * hd:(mp + 1) * hd], k_ref[:, mp * hd:(mp + 1) * hd]) * scale
            s = jnp.where(keep, s, -jnp.inf)
            m_new, l_new, acc_new = _softmax_step(s, v, m_scr[mp], l_scr[mp], acc_scr[mp])
            m_scr[mp] = m_new
            l_scr[mp] = l_new
            acc_scr[mp] = acc_new

    @pl.when(ki == qi)
    def _():
        lam = _lambda(lq1_ref[...], lk1_ref[...], lq2_ref[...], lk2_ref[...], lam_init)
        o = acc_scr[0] / l_scr[0] - lam * (acc_scr[1] / l_scr[1])
        o_ref[...] = (_rms(o, sw_ref[...]) * (1.0 - lam_init)).astype(BF16)


def _pattn_call(q, kb, vb, lq1, lk1, lq2, lk2, subw, nbatch, lam_init):
    m, d = q.shape
    vd = subw.shape[1]
    hd = lq1.shape[1]
    nh = d // vd
    t = m // nbatch
    tq = _tile(t, ATTN_TILE)
    nq = t // tq
    kern = functools.partial(_pattn_kernel, hd=hd, tq=tq, scale=hd ** -0.5, lam_init=lam_init)
    vec = pl.BlockSpec((1, hd), lambda b, h, i, j: (0, 0))
    return pl.pallas_call(
        kern,
        grid=(nbatch, nh, nq, nq),
        in_specs=[pl.BlockSpec((tq, vd), lambda b, h, i, j: (b * nq + i, h)),
                  pl.BlockSpec((tq, vd), lambda b, h, i, j: (b * nq + jnp.minimum(i, j), h)),
                  pl.BlockSpec((tq, vd), lambda b, h, i, j: (b * nq + jnp.minimum(i, j), h)),
                  vec, vec, vec, vec,
                  pl.BlockSpec((1, vd), lambda b, h, i, j: (0, 0))],
        out_specs=pl.BlockSpec((tq, vd), lambda b, h, i, j: (b * nq + i, h)),
        out_shape=jax.ShapeDtypeStruct((m, d), BF16),
        scratch_shapes=[pltpu.VMEM((2, tq, 1), F32), pltpu.VMEM((2, tq, 1), F32), pltpu.VMEM((2, tq, vd), F32)],
        compiler_params=_params("parallel", "parallel", "parallel", "arbitrary"),
        name="prompt_attn",
    )(q, kb, vb, lq1, lk1, lq2, lk2, subw)


def _sattn_kernel(pt_ref, q_ref, *refs, npg, nh, nq, hd, vd, scale, lam_init):
    kp_refs = refs[:npg]
    vp_refs = refs[npg:2 * npg]
    kn_ref, vn_ref, lq1_ref, lk1_ref, lq2_ref, lk2_ref, sw_ref, o_ref, m_scr, l_scr, acc_scr = refs[2 * npg:]
    s_id = pl.program_id(1)
    q = q_ref[0]

    @pl.when(s_id == 0)
    def _():
        m_scr[...] = jnp.full(m_scr.shape, -jnp.inf, F32)
        l_scr[...] = jnp.zeros(l_scr.shape, F32)
        acc_scr[...] = jnp.zeros(acc_scr.shape, F32)

    def update(k, v, keep):
        s = _dot_nt(q, k) * scale
        if keep is not None:
            s = jnp.where(keep, s, -jnp.inf)
        m_new, l_new, acc_new = _softmax_step(s, v, m_scr[...], l_scr[...], acc_scr[...])
        m_scr[...] = m_new
        l_scr[...] = l_new
        acc_scr[...] = acc_new

    for j in range(npg):
        update(kp_refs[j][0].astype(BF16), vp_refs[j][0].astype(BF16), None)

    @pl.when(s_id == pl.num_programs(1) - 1)
    def _():
        nk = kn_ref.shape[1]
        jrow = lax.broadcasted_iota(jnp.int32, (q.shape[0], nk), 0)
        tcol = lax.broadcasted_iota(jnp.int32, (q.shape[0], nk), 1)
        update(kn_ref[0], vn_ref[0], tcol <= lax.rem(jrow, jnp.full_like(jrow, nq)))
        lam = _lambda(lq1_ref[...], lk1_ref[...], lq2_ref[...], lk2_ref[...], lam_init)
        on = acc_scr[...] / l_scr[...]
        for h in range(nh):
            r0 = h * 2 * nq
            o0 = on[r0:r0 + nq, h * vd:(h + 1) * vd]
            o1 = on[r0 + nq:r0 + 2 * nq, h * vd:(h + 1) * vd]
            o = o0 - lam * o1
            o_ref[0, :, h * vd:(h + 1) * vd] = _rms(o, sw_ref[...]) * (1.0 - lam_init)


def _sattn_call(page_table, qrows, cache_k, cache_v, k_new, v_new, lq1, lk1, lq2, lk2, subw, nq, lam_init):
    nbatch, jr, d = qrows.shape
    page = cache_k.shape[1]
    vd = subw.shape[1]
    hd = lq1.shape[1]
    nh = d // vd
    npages = page_table.shape[1]
    npg = _tile(npages, 4)
    nk = k_new.shape[1]
    kern = functools.partial(_sattn_kernel, npg=npg, nh=nh, nq=nq, hd=hd, vd=vd, scale=hd ** -0.5,
                             lam_init=lam_init)

    def page_spec(j):
        return pl.BlockSpec((1, page, d), lambda b, s, pt: (pt[b, s * npg + j], 0, 0))

    vec = pl.BlockSpec((1, hd), lambda b, s, pt: (0, 0))
    grid_spec = pltpu.PrefetchScalarGridSpec(
        num_scalar_prefetch=1,
        grid=(nbatch, npages // npg),
        in_specs=([pl.BlockSpec((1, jr, d), lambda b, s, pt: (b, 0, 0))]
                  + [page_spec(j) for j in range(npg)] + [page_spec(j) for j in range(npg)]
                  + [pl.BlockSpec((1, nk, d), lambda b, s, pt: (b, 0, 0)),
                     pl.BlockSpec((1, nk, d), lambda b, s, pt: (b, 0, 0)),
                     vec, vec, vec, vec,
                     pl.BlockSpec((1, vd), lambda b, s, pt: (0, 0))]),
        out_specs=pl.BlockSpec((1, nq, d), lambda b, s, pt: (b, 0, 0)),
        scratch_shapes=[pltpu.VMEM((jr, 1), F32), pltpu.VMEM((jr, 1), F32), pltpu.VMEM((jr, d), F32)],
    )
    return pl.pallas_call(
        kern,
        grid_spec=grid_spec,
        out_shape=jax.ShapeDtypeStruct((nbatch, nq, d), F32),
        compiler_params=_params("parallel", "arbitrary"),
        name="sample_attn",
    )(page_table, qrows, *([cache_k] * npg), *([cache_v] * npg), k_new, v_new, lq1, lk1, lq2, lk2, subw)


def _to_groups(h, nbatch, g, pad_to):
    m, d = h.shape
    t = m // nbatch
    gc = d // g
    h4 = h.reshape(nbatch, t, g, gc)
    if pad_to > t:
        h4 = jnp.pad(h4, ((0, 0), (0, pad_to - t), (0, 0), (0, 0)))
    nc = pad_to // SSM_CHUNK
    h5 = h4.reshape(nbatch, nc, SSM_CHUNK, g, gc)
    return jnp.transpose(h5, (3, 0, 1, 2, 4)).reshape(g, nbatch * nc, SSM_CHUNK * gc)


def _from_groups(yg, nbatch, t):
    g, r, w = yg.shape
    nc = r // nbatch
    gc = w // SSM_CHUNK
    y5 = yg.reshape(g, nbatch, nc, SSM_CHUNK, gc)
    y4 = jnp.transpose(y5, (1, 2, 3, 0, 4)).reshape(nbatch, nc * SSM_CHUNK, g * gc)
    return y4[:, :t].reshape(nbatch * t, g * gc)


def _run_group(x3, mods, kvmod, s_re0, s_im0, attend, p, nb):
    nbatch, t, d = x3.shape
    m = nbatch * t
    x = x3.reshape(m, d)
    depth = p['norm_w'].shape[0]
    n_a = p['lam_re'].shape[0]
    g, n = p['lam_re'].shape[1:]
    t_pad = -(-t // SSM_CHUNK) * SSM_CHUNK
    nc = t_pad // SSM_CHUNK
    tv = SSM_CHUNK if nc > 1 else t
    new_re, new_im = [], []
    k = v = kb = vb = None
    for l in range(depth):
        sh1, sc1, g1, sh2, sc2, g2 = mods[l]
        nw1 = p['norm_w'][l, 0][None]
        if l < n_a:
            tabs = _ssm_tables(p['lam_re'][l], p['lam_im'][l], p['log_step'][l], p['b_re'][l], p['b_im'][l],
                               p['c_re'][l], p['c_im'][l], tv, nc)
            h = _norm_mod_call(x, nw1, sh1, sc1, nb)
            s0 = jnp.concatenate([jnp.transpose(s_re0[l], (1, 0, 2)), jnp.transpose(s_im0[l], (1, 0, 2))], -1)
            yg, sf = _ssm_call(_to_groups(h, nbatch, g, t_pad), *tabs, s0.astype(F32), nbatch)
            new_re.append(jnp.transpose(sf[..., :n], (1, 0, 2)))
            new_im.append(jnp.transpose(sf[..., n:], (1, 0, 2)))
            x = _glu_call(x, _from_groups(yg, nbatch, t), nw1, sh1, sc1, p['ssm_d'][l][None], p['glu_w'][l], g1, nb)
        else:
            j = l - n_a
            lam_init = 0.8 - 0.6 * math.exp(-0.3 * l)
            q = _qproj_call(x, nw1, sh1, sc1, p['attn_wq'][j], p['q_norm_w'][j][None], nb)
            o = attend(q, kb, vb, p['lambda_q1'][j][None], p['lambda_k1'][j][None], p['lambda_q2'][j][None],
                       p['lambda_k2'][j][None], p['subln_w'][j][None], lam_init)
            x = _oproj_call(o, p['attn_wo'][j], x, g1, nb)
        x = _mlp_call(x, p['norm_w'][l, 1][None], sh2, sc2, g2, p['mlp_up'][l], p['mlp_down'][l], nb)
        if l == n_a - 1:
            k, v, kb, vb = _kvproj_call(x, p['kv_norm_w'][None], kvmod[0], kvmod[1], p['kv_w'], p['k_norm_w'][None],
                                        nb)
    return x.reshape(nbatch, t, d), jnp.stack(new_re), jnp.stack(new_im), k, v


def kernel(x_prompt, x_sample, state_ssm_re, state_ssm_im, cache_k, cache_v, page_table, c_prompt, c_sample, ada_w, ada_b, norm_w, mlp_up, mlp_down, ssm_lambda_re, ssm_lambda_im, ssm_log_step, ssm_b_re, ssm_b_im, ssm_c_re, ssm_c_im, ssm_d, glu_w, kv_ada_w, kv_ada_b, kv_norm_w, kv_w, k_norm_w, attn_wq, q_norm_w, lambda_q1, lambda_k1, lambda_q2, lambda_k2, subln_w, attn_wo):
    bp, tp, d = x_prompt.shape
    bs, ts, _ = x_sample.shape
    depth = ada_w.shape[0]
    n_a = ssm_lambda_re.shape[0]
    nh, vd = cache_v.shape[2], cache_v.shape[3]
    hd = q_norm_w.shape[1]

    p = {
        'norm_w': norm_w, 'mlp_up': mlp_up.astype(BF16), 'mlp_down': mlp_down.astype(BF16),
        'lam_re': ssm_lambda_re, 'lam_im': ssm_lambda_im, 'log_step': ssm_log_step,
        'b_re': ssm_b_re.astype(F32), 'b_im': ssm_b_im.astype(F32),
        'c_re': ssm_c_re.astype(F32), 'c_im': ssm_c_im.astype(F32),
        'ssm_d': ssm_d, 'glu_w': glu_w.astype(BF16), 'kv_norm_w': kv_norm_w, 'kv_w': kv_w.astype(BF16),
        'k_norm_w': k_norm_w, 'attn_wq': attn_wq.astype(BF16), 'q_norm_w': q_norm_w,
        'lambda_q1': lambda_q1, 'lambda_k1': lambda_k1, 'lambda_q2': lambda_q2, 'lambda_k2': lambda_k2,
        'subln_w': subln_w, 'attn_wo': attn_wo.astype(BF16),
    }

    nc_rows = bp + bs
    c_all = jnp.concatenate([c_prompt, c_sample], axis=0)
    c_all = jnp.pad(c_all, ((0, -nc_rows % 16), (0, 0))).astype(BF16)
    mod_all = _ada(c_all, ada_w, ada_b[:, None, :])
    kvmod_all = _ada(c_all, kv_ada_w[None], kv_ada_b[None, None, :])[0]

    def split_mods(lo, hi, rep):
        def shape(a):
            return a[:, None, :] if rep == 1 else jnp.repeat(a, rep, axis=0)[None]
        mods = [[shape(mod_all[l, lo:hi, i * d:(i + 1) * d]) for i in range(6)] for l in range(depth)]
        kvmod = [shape(kvmod_all[lo:hi, i * d:(i + 1) * d]) for i in range(2)]
        return mods, kvmod

    mods_p, kvmod_p = split_mods(0, bp, 1)
    zeros = jnp.zeros((n_a, bp) + ssm_lambda_re.shape[1:], F32)

    def attend_prompt(q, kb, vb, lq1, lk1, lq2, lk2, subw, lam_init):
        return _pattn_call(q, kb, vb, lq1, lk1, lq2, lk2, subw, bp, lam_init)

    y_p, re_p, im_p, k_p, v_p = _run_group(x_prompt, mods_p, kvmod_p, zeros, zeros, attend_prompt, p, bp)

    mods_s, kvmod_s = split_mods(bp, bp + bs, ts)
    page = cache_k.shape[1]
    ck = cache_k.reshape(cache_k.shape[0], page, d)
    cv = cache_v.reshape(cache_v.shape[0], page, d)
    eye = jnp.eye(2 * nh, dtype=BF16)

    def attend_sample(q, kb, vb, lq1, lk1, lq2, lk2, subw, lam_init):
        q5 = jnp.transpose(q.reshape(bs, ts, 2 * nh, hd), (0, 2, 1, 3))
        qrows = jnp.einsum('bcid,cC->bciCd', q5, eye).reshape(bs, 2 * nh * ts, d)
        pad = ((0, 0), (0, page - ts), (0, 0))
        k_new = jnp.pad(kb.reshape(bs, ts, d), pad)
        v_new = jnp.pad(vb.reshape(bs, ts, d), pad)
        o = _sattn_call(page_table, qrows, ck, cv, k_new, v_new, lq1, lk1, lq2, lk2, subw, ts, lam_init)
        return o.reshape(bs * ts, d).astype(BF16)

    y_s, re_s, im_s, k_s, v_s = _run_group(x_sample, mods_s, kvmod_s, state_ssm_re, state_ssm_im, attend_sample,
                                           p, 1)

    return (y_p, y_s, re_p, im_p,
            k_p.reshape(bp, tp, nh, 2 * hd), v_p.reshape(bp, tp, nh, vd),
            re_s, im_s,
            k_s.reshape(bs, ts, nh, 2 * hd), v_s.reshape(bs, ts, nh, vd))
```

```python
import functools
import math

import jax
import jax.numpy as jnp
from jax import lax
from jax.experimental import pallas as pl
from jax.experimental.pallas import tpu as pltpu

F32 = jnp.float32
BF16 = jnp.bfloat16
HI = lax.Precision.HIGHEST

NORM_EPS = 1e-6
SSM_CHUNK = 16
MIB = 1024 * 1024
VMEM_LIMIT_BYTES = 52 * MIB
ATTN_TILE = 512


def _params(*sem):
    return pltpu.CompilerParams(dimension_semantics=sem, vmem_limit_bytes=VMEM_LIMIT_BYTES)


def _tile(dim, pref):
    t = min(dim, pref)
    assert dim % t == 0, (dim, pref)
    return t


def _dot(a, b):
    return jnp.dot(a, b, preferred_element_type=F32)


def _dot_nt(a, b):
    return lax.dot_general(a, b, (((1,), (1,)), ((), ())), preferred_element_type=F32)


def _rms(x, w):
    return x * lax.rsqrt(jnp.mean(x * x, axis=-1, keepdims=True) + NORM_EPS) * w


def _norm_mod(x, nw, sh, sc):
    return _rms(x, nw) * (1.0 + sc) + sh


def _chunk_rms(y, w, width):
    parts = []
    for c in range(y.shape[-1] // width):
        parts.append(_rms(y[:, c * width:(c + 1) * width], w))
    return parts[0] if len(parts) == 1 else jnp.concatenate(parts, axis=-1)


def _ada_kernel(c_ref, w_ref, b_ref, o_ref):
    o_ref[0] = _dot(c_ref[...], w_ref[0].astype(BF16)) + b_ref[0]


def _ada(c, w, b):
    nl, d, n = w.shape
    r = c.shape[0]
    tn = _tile(n, 512)
    return pl.pallas_call(
        _ada_kernel,
        grid=(nl, n // tn),
        in_specs=[pl.BlockSpec((r, d), lambda l, j: (0, 0)),
                  pl.BlockSpec((1, d, tn), lambda l, j: (l, 0, j)),
                  pl.BlockSpec((1, 1, tn), lambda l, j: (l, 0, j))],
        out_specs=pl.BlockSpec((1, r, tn), lambda l, j: (l, 0, j)),
        out_shape=jax.ShapeDtypeStruct((nl, r, n), F32),
        compiler_params=_params("parallel", "parallel"),
        name="ada",
    )(c, w, b)


class _Rows:
    def __init__(self, m, nb, tm_pref):
        self.m = m
        self.nb = nb
        rows_per_nb = m // nb
        self.tm = _tile(rows_per_nb, tm_pref)
        self.tiles_per_nb = rows_per_nb // self.tm
        self.nt = m // self.tm

    def mod_spec(self, r, width, col=None):
        tpn = self.tiles_per_nb
        if col is None:
            return pl.BlockSpec((1, r, width), lambda i, j: (i // tpn, 0, 0))
        return pl.BlockSpec((1, r, width), lambda i, j: (i // tpn, 0, j))


def _norm_mod_kernel(x_ref, nw_ref, sh_ref, sc_ref, o_ref):
    o_ref[...] = _norm_mod(x_ref[...], nw_ref[...], sh_ref[0], sc_ref[0]).astype(BF16)


def _norm_mod_call(x, nw, sh, sc, nb):
    m, d = x.shape
    rows = _Rows(m, nb, 512)
    tm, r = rows.tm, sh.shape[1]
    tpn = rows.tiles_per_nb
    return pl.pallas_call(
        _norm_mod_kernel,
        grid=(rows.nt,),
        in_specs=[pl.BlockSpec((tm, d), lambda i: (i, 0)),
                  pl.BlockSpec((1, d), lambda i: (0, 0)),
                  pl.BlockSpec((1, r, d), lambda i: (i // tpn, 0, 0)),
                  pl.BlockSpec((1, r, d), lambda i: (i // tpn, 0, 0))],
        out_specs=pl.BlockSpec((tm, d), lambda i: (i, 0)),
        out_shape=jax.ShapeDtypeStruct((m, d), BF16),
        compiler_params=_params("parallel"),
        name="norm_mod",
    )(x, nw, sh, sc)


def _cmul(a_ref_val, s, half):
    return a_ref_val[0:1, :] * s + a_ref_val[1:2, :] * pltpu.roll(s, half, axis=1)


def _ssm_kernel(x_ref, toep_ref, wst_ref, wout_ref, apow_ref, s0_ref, y_ref, sf_ref, *, gb, nb, nc, half):
    r = nb * nc
    nsteps = nc.bit_length() - 1
    row = lax.broadcasted_iota(jnp.int32, (r, 2 * half), 0)
    chunk = row & (nc - 1)

    def body(gi, carry):
        x = x_ref[gi]
        y = _dot(x, toep_ref[gi])
        p = _dot(x, wst_ref[gi])
        s0 = s0_ref[gi]
        s0_rows = jnp.zeros((r, 2 * half), F32)
        for b in range(nb):
            s0_rows = jnp.where(row == b * nc, s0[b:b + 1, :], s0_rows)
        p = p + _cmul(apow_ref[gi, 0], s0_rows, half)
        for k in range(nsteps):
            d = 1 << k
            shifted = jnp.where(chunk >= d, pltpu.roll(p, d, axis=0), 0.0)
            p = p + _cmul(apow_ref[gi, k], shifted, half)
        s_in = s0_rows if nc == 1 else jnp.where(chunk == 0, s0_rows, pltpu.roll(p, 1, axis=0))
        y_ref[gi] = y + _dot(s_in.astype(BF16), wout_ref[gi])
        for b in range(nb):
            last = b * nc + nc - 1
            sf_ref[gi, b:b + 1, :] = p[last:last + 1, :]
        return carry

    lax.fori_loop(0, gb, body, 0)


def _ssm_call(xg, toep, wst, wout, apow, s0, nb):
    g, r, w = xg.shape
    nc = r // nb
    assert nc & (nc - 1) == 0
    n2 = wst.shape[2]
    ns = apow.shape[1]
    gb = _tile(g, 8)
    kern = functools.partial(_ssm_kernel, gb=gb, nb=nb, nc=nc, half=n2 // 2)
    return pl.pallas_call(
        kern,
        grid=(g // gb,),
        in_specs=[pl.BlockSpec((gb, r, w), lambda i: (i, 0, 0)),
                  pl.BlockSpec((gb, w, w), lambda i: (i, 0, 0)),
                  pl.BlockSpec((gb, w, n2), lambda i: (i, 0, 0)),
                  pl.BlockSpec((gb, n2, w), lambda i: (i, 0, 0)),
                  pl.BlockSpec((gb, ns, 2, n2), lambda i: (i, 0, 0, 0)),
                  pl.BlockSpec((gb, nb, n2), lambda i: (i, 0, 0))],
        out_specs=[pl.BlockSpec((gb, r, w), lambda i: (i, 0, 0)),
                   pl.BlockSpec((gb, nb, n2), lambda i: (i, 0, 0))],
        out_shape=[jax.ShapeDtypeStruct((g, r, w), F32),
                   jax.ShapeDtypeStruct((g, nb, n2), F32)],
        compiler_params=_params("parallel"),
        name="ssm",
    )(xg, toep, wst, wout, apow, s0)


def _ssm_tables(lam_re, lam_im, log_step, b_re, b_im, c_re, c_im, tv, nc):
    L = SSM_CHUNK
    g, n = lam_re.shape
    gc = b_re.shape[2]
    lr, li = lam_re.astype(F32), lam_im.astype(F32)
    dt = jnp.exp(log_step.astype(F32))[:, None]
    mag = jnp.exp(lr * dt)
    ar, ai = mag * jnp.cos(li * dt), mag * jnp.sin(li * dt)
    den = lr * lr + li * li
    xr = ar - 1.0
    f_re = (xr * lr + ai * li) / den
    f_im = (ai * lr - xr * li) / den
    bb_re = f_re[..., None] * b_re - f_im[..., None] * b_im
    bb_im = f_re[..., None] * b_im + f_im[..., None] * b_re
    j = jnp.arange(L + 1, dtype=F32)[:, None, None]
    pmag = jnp.exp(j * (lr * dt))
    pw_re, pw_im = pmag * jnp.cos(j * (li * dt)), pmag * jnp.sin(j * (li * dt))
    pb_re = pw_re[..., None] * bb_re - pw_im[..., None] * bb_im
    pb_im = pw_re[..., None] * bb_im + pw_im[..., None] * bb_re
    kj = (jnp.einsum('gcn,jgnd->jgcd', c_re, pb_re[:L], precision=HI)
          - jnp.einsum('gcn,jgnd->jgcd', c_im, pb_im[:L], precision=HI))
    tk = jnp.arange(L)
    lag = tk[None, :] - tk[:, None]
    t5 = jnp.where((lag >= 0)[:, :, None, None, None], kj[jnp.clip(lag, 0, L - 1)], 0.0)
    toep = jnp.transpose(t5, (2, 0, 4, 1, 3)).reshape(g, L * gc, L * gc)
    sel = jnp.clip(tv - 1 - tk, 0, L)
    valid = (tk < tv)[:, None, None, None]
    ws_re = jnp.where(valid, pb_re[sel], 0.0)
    ws_im = jnp.where(valid, pb_im[sel], 0.0)
    wst = jnp.concatenate([jnp.transpose(ws_re, (1, 0, 3, 2)), jnp.transpose(ws_im, (1, 0, 3, 2))], axis=-1)
    wst = wst.reshape(g, L * gc, 2 * n)
    cp_re = c_re[None] * pw_re[1:, :, None, :] - c_im[None] * pw_im[1:, :, None, :]
    cp_im = c_re[None] * pw_im[1:, :, None, :] + c_im[None] * pw_re[1:, :, None, :]
    wout = jnp.concatenate([jnp.transpose(cp_re, (1, 3, 0, 2)), -jnp.transpose(cp_im, (1, 3, 0, 2))], axis=1)
    wout = wout.reshape(g, 2 * n, L * gc)
    are, aim = pw_re[tv], pw_im[tv]
    steps = []
    for _ in range(max(nc.bit_length() - 1, 1)):
        steps.append(jnp.stack([jnp.concatenate([are, are], -1), jnp.concatenate([-aim, aim], -1)], axis=1))
        are, aim = are * are - aim * aim, 2.0 * are * aim
    apow = jnp.stack(steps, axis=1)
    return toep.astype(BF16), wst.astype(BF16), wout.astype(BF16), apow


def _glu_kernel(x_ref, y_ref, nw_ref, sh_ref, sc_ref, d_ref, wv_ref, wg_ref, xr_ref, g_ref, o_ref, a_scr):
    @pl.when(pl.program_id(1) == 0)
    def _():
        h = _norm_mod(x_ref[...], nw_ref[...], sh_ref[0], sc_ref[0])
        a_scr[...] = jax.nn.gelu(y_ref[...] + d_ref[...] * h).astype(BF16)

    a = a_scr[...]
    val = _dot(a, wv_ref[...])
    gate = _dot(a, wg_ref[...])
    o_ref[...] = xr_ref[...] + g_ref[0] * (val / (1.0 + jnp.exp(-gate)))


def _glu_call(x, y, nw, sh, sc, dskip, w, g1, nb):
    m, d = x.shape
    rows = _Rows(m, nb, 512)
    tm, r = rows.tm, sh.shape[1]
    tn = _tile(d, 512)
    nn = d // tn
    return pl.pallas_call(
        _glu_kernel,
        grid=(rows.nt, nn),
        in_specs=[pl.BlockSpec((tm, d), lambda i, j: (i, 0)),
                  pl.BlockSpec((tm, d), lambda i, j: (i, 0)),
                  pl.BlockSpec((1, d), lambda i, j: (0, 0)),
                  rows.mod_spec(r, d), rows.mod_spec(r, d),
                  pl.BlockSpec((1, d), lambda i, j: (0, 0)),
                  pl.BlockSpec((d, tn), lambda i, j: (0, j)),
                  pl.BlockSpec((d, tn), lambda i, j: (0, j + nn)),
                  pl.BlockSpec((tm, tn), lambda i, j: (i, j)),
                  rows.mod_spec(r, tn, col=True)],
        out_specs=pl.BlockSpec((tm, tn), lambda i, j: (i, j)),
        out_shape=jax.ShapeDtypeStruct((m, d), F32),
        scratch_shapes=[pltpu.VMEM((tm, d), BF16)],
        compiler_params=_params("parallel", "arbitrary"),
        name="glu",
    )(x, y, nw, sh, sc, dskip, w, w, x, g1)


def _mlp_kernel(x_ref, nw_ref, sh_ref, sc_ref, g_ref, up_ref, dn_ref, o_ref, h_scr, *, nf):
    f = pl.program_id(1)

    @pl.when(f == 0)
    def _():
        h_scr[...] = _norm_mod(x_ref[...], nw_ref[...], sh_ref[0], sc_ref[0]).astype(BF16)

    a = jnp.maximum(_dot(h_scr[...], up_ref[...]), 0.0)
    part = _dot((a * a).astype(BF16), dn_ref[...])

    @pl.when(f == 0)
    def _():
        o_ref[...] = part

    @pl.when(f > 0)
    def _():
        o_ref[...] += part

    @pl.when(f == nf - 1)
    def _():
        o_ref[...] = x_ref[...] + g_ref[0] * o_ref[...]


def _mlp_call(x, nw, sh, sc, g2, up, dn, nb):
    m, d = x.shape
    ff = up.shape[1]
    rows = _Rows(m, nb, 512)
    tm, r = rows.tm, sh.shape[1]
    tf = _tile(ff, 1024)
    nf = ff // tf
    return pl.pallas_call(
        functools.partial(_mlp_kernel, nf=nf),
        grid=(rows.nt, nf),
        in_specs=[pl.BlockSpec((tm, d), lambda i, j: (i, 0)),
                  pl.BlockSpec((1, d), lambda i, j: (0, 0)),
                  rows.mod_spec(r, d), rows.mod_spec(r, d), rows.mod_spec(r, d),
                  pl.BlockSpec((d, tf), lambda i, j: (0, j)),
                  pl.BlockSpec((tf, d), lambda i, j: (j, 0))],
        out_specs=pl.BlockSpec((tm, d), lambda i, j: (i, 0)),
        out_shape=jax.ShapeDtypeStruct((m, d), F32),
        scratch_shapes=[pltpu.VMEM((tm, d), BF16)],
        compiler_params=_params("parallel", "arbitrary"),
        name="mlp",
    )(x, nw, sh, sc, g2, up, dn)


def _mod_rows(ref, r0, ts):
    return ref[0] if ref.shape[1] == 1 else ref[0, r0:r0 + ts, :]


def _qproj_kernel(x_ref, nw_ref, sh_ref, sc_ref, w_ref, cw_ref, q_ref, *, hd, ts, qscale):
    cw = cw_ref[...] * qscale
    for r0 in range(0, x_ref.shape[0], ts):
        h = _norm_mod(x_ref[r0:r0 + ts, :], nw_ref[...], _mod_rows(sh_ref, r0, ts), _mod_rows(sc_ref, r0, ts))
        q = _dot(h.astype(BF16), w_ref[...])
        for c in range(q.shape[1] // hd):
            q_ref[r0:r0 + ts, c * hd:(c + 1) * hd] = _rms(q[:, c * hd:(c + 1) * hd], cw).astype(BF16)


def _qproj_call(x, nw, sh, sc, wq, qnw, nb, qscale):
    m, d = x.shape
    n = wq.shape[1]
    hd = qnw.shape[1]
    rows = _Rows(m, nb, 512)
    tm, r = rows.tm, sh.shape[1]
    tpn = rows.tiles_per_nb
    mod = pl.BlockSpec((1, r, d), lambda i: (i // tpn, 0, 0))
    return pl.pallas_call(
        functools.partial(_qproj_kernel, hd=hd, ts=min(tm, 256), qscale=qscale),
        grid=(rows.nt,),
        in_specs=[pl.BlockSpec((tm, d), lambda i: (i, 0)),
                  pl.BlockSpec((1, d), lambda i: (0, 0)),
                  mod, mod,
                  pl.BlockSpec((d, n), lambda i: (0, 0)),
                  pl.BlockSpec((1, hd), lambda i: (0, 0))],
        out_specs=pl.BlockSpec((tm, n), lambda i: (i, 0)),
        out_shape=jax.ShapeDtypeStruct((m, n), BF16),
        compiler_params=_params("parallel"),
        name="q_proj",
    )(x, nw, sh, sc, wq, qnw)


def _kvproj_kernel(x_ref, nw_ref, sh_ref, sc_ref, wk_ref, wv_ref, cw_ref, k_ref, v_ref, kb_ref, vb_ref, h_scr,
                   *, hd):
    @pl.when(pl.program_id(1) == 0)
    def _():
        h_scr[...] = _norm_mod(x_ref[...], nw_ref[...], sh_ref[0], sc_ref[0]).astype(BF16)

    h = h_scr[...]
    k = _chunk_rms(_dot(h, wk_ref[...]), cw_ref[...], hd)
    v = _dot(h, wv_ref[...])
    k_ref[...] = k
    v_ref[...] = v
    kb_ref[...] = k.astype(BF16)
    vb_ref[...] = v.astype(BF16)


def _kvproj_call(x, nw, sh, sc, w, knw, nb):
    m, d = x.shape
    nk = w.shape[1] // 2
    hd = knw.shape[1]
    rows = _Rows(m, nb, 512)
    tm, r = rows.tm, sh.shape[1]
    tn = _tile(nk, 512)
    nn = nk // tn
    out_spec = pl.BlockSpec((tm, tn), lambda i, j: (i, j))
    return pl.pallas_call(
        functools.partial(_kvproj_kernel, hd=hd),
        grid=(rows.nt, nn),
        in_specs=[pl.BlockSpec((tm, d), lambda i, j: (i, 0)),
                  pl.BlockSpec((1, d), lambda i, j: (0, 0)),
                  rows.mod_spec(r, d), rows.mod_spec(r, d),
                  pl.BlockSpec((d, tn), lambda i, j: (0, j)),
                  pl.BlockSpec((d, tn), lambda i, j: (0, j + nn)),
                  pl.BlockSpec((1, hd), lambda i, j: (0, 0))],
        out_specs=[out_spec, out_spec, out_spec, out_spec],
        out_shape=[jax.ShapeDtypeStruct((m, nk), F32), jax.ShapeDtypeStruct((m, nk), F32),
                   jax.ShapeDtypeStruct((m, nk), BF16), jax.ShapeDtypeStruct((m, nk), BF16)],
        scratch_shapes=[pltpu.VMEM((tm, d), BF16)],
        compiler_params=_params("parallel", "arbitrary"),
        name="kv_proj",
    )(x, nw, sh, sc, w, w, knw)


def _oproj_kernel(o_ref, w_ref, x_ref, g_ref, y_ref):
    y_ref[...] = x_ref[...] + g_ref[0] * _dot(o_ref[...], w_ref[...])


def _oproj_call(o, wo, x, g1, nb):
    m, d = x.shape
    rows = _Rows(m, nb, 512)
    tm, r = rows.tm, g1.shape[1]
    tpn = rows.tiles_per_nb
    return pl.pallas_call(
        _oproj_kernel,
        grid=(rows.nt,),
        in_specs=[pl.BlockSpec((tm, o.shape[1]), lambda i: (i, 0)),
                  pl.BlockSpec((o.shape[1], d), lambda i: (0, 0)),
                  pl.BlockSpec((tm, d), lambda i: (i, 0)),
                  pl.BlockSpec((1, r, d), lambda i: (i // tpn, 0, 0))],
        out_specs=pl.BlockSpec((tm, d), lambda i: (i, 0)),
        out_shape=jax.ShapeDtypeStruct((m, d), F32),
        compiler_params=_params("parallel"),
        name="o_proj",
    )(o, wo, x, g1)


def _lambda(lq1, lk1, lq2, lk2, lam_init):
    return (jnp.exp(jnp.sum(lq1 * lk1, axis=-1, keepdims=True))
            - jnp.exp(jnp.sum(lq2 * lk2, axis=-1, keepdims=True)) + lam_init)


def _pattn_kernel(q_ref, k_ref, v_ref, lq1_ref, lk1_ref, lq2_ref, lk2_ref, sw_ref, o_ref, m_scr, l_scr, acc_scr,
                  *, hd, tq, lam_init):
    qi = pl.program_id(2)
    m_scr[...] = jnp.full(m_scr.shape, -jnp.inf, F32)
    l_scr[...] = jnp.zeros(l_scr.shape, F32)
    acc_scr[...] = jnp.zeros(acc_scr.shape, F32)

    def block(kstart, masked):
        k = k_ref[pl.ds(kstart, tq), :]
        v = v_ref[pl.ds(kstart, tq), :]
        old = [(m_scr[mp], l_scr[mp], acc_scr[mp]) for mp in range(2)]
        s = [_dot_nt(q_ref[:, mp * hd:(mp + 1) * hd], k[:, mp * hd:(mp + 1) * hd]) for mp in range(2)]
        if masked:
            rowi = lax.broadcasted_iota(jnp.int32, (tq, tq), 0)
            coli = lax.broadcasted_iota(jnp.int32, (tq, tq), 1)
            s = [jnp.where(coli <= rowi, x, -jnp.inf) for x in s]
        new = []
        for mp in range(2):
            m_prev, l_prev, acc_prev = old[mp]
            m_new = jnp.maximum(m_prev, jnp.max(s[mp], axis=-1, keepdims=True))
            alpha = jnp.exp2(m_prev - m_new)
            p = jnp.exp2(s[mp] - m_new)
            new.append((m_new, alpha * l_prev + jnp.sum(p, axis=-1, keepdims=True),
                        alpha * acc_prev + _dot(p.astype(BF16), v)))
        for mp in range(2):
            m_scr[mp], l_scr[mp], acc_scr[mp] = new[mp]

    def body(j, carry):
        block(pl.multiple_of(j * tq, tq), False)
        return carry

    lax.fori_loop(0, qi, body, 0)
    block(pl.multiple_of(qi * tq, tq), True)
    lam = _lambda(lq1_ref[...], lk1_ref[...], lq2_ref[...], lk2_ref[...], lam_init)
    o = acc_scr[0] / l_scr[0] - lam * (acc_scr[1] / l_scr[1])
    o_ref[...] = (_rms(o, sw_ref[...]) * (1.0 - lam_init)).astype(BF16)


def _pattn_call(q, kb, vb, lq1, lk1, lq2, lk2, subw, nbatch, lam_init):
    m, d = q.shape
    vd = subw.shape[1]
    hd = lq1.shape[1]
    nh = d // vd
    t = m // nbatch
    tq = _tile(t, ATTN_TILE)
    nq = t // tq
    kern = functools.partial(_pattn_kernel, hd=hd, tq=tq, lam_init=lam_init)
    vec = pl.BlockSpec((1, hd), lambda b, h, i: (0, 0))
    return pl.pallas_call(
        kern,
        grid=(nbatch, nh, nq),
        in_specs=[pl.BlockSpec((tq, vd), lambda b, h, i: (b * nq + i, h)),
                  pl.BlockSpec((t, vd), lambda b, h, i: (b, h)),
                  pl.BlockSpec((t, vd), lambda b, h, i: (b, h)),
                  vec, vec, vec, vec,
                  pl.BlockSpec((1, vd), lambda b, h, i: (0, 0))],
        out_specs=pl.BlockSpec((tq, vd), lambda b, h, i: (b * nq + i, h)),
        out_shape=jax.ShapeDtypeStruct((m, d), BF16),
        scratch_shapes=[pltpu.VMEM((2, tq, 1), F32), pltpu.VMEM((2, tq, 1), F32), pltpu.VMEM((2, tq, vd), F32)],
        compiler_params=_params("parallel", "parallel", "parallel"),
        name="prompt_attn",
    )(q, kb, vb, lq1, lk1, lq2, lk2, subw)


def _idiv(x, n):
    if n & (n - 1) == 0:
        return lax.shift_right_logical(x, jnp.full_like(x, n.bit_length() - 1))
    return lax.div(x, jnp.full_like(x, n))


def _irem(x, n):
    if n & (n - 1) == 0:
        return x & (n - 1)
    return lax.rem(x, jnp.full_like(x, n))


def _sattn_kernel(pt_ref, qw_ref, *refs, npg, nh, nq, ncol, lam_init):
    kp_refs = refs[:npg]
    vp_refs = refs[npg:2 * npg]
    kn_ref, vn_ref, lq1_ref, lk1_ref, lq2_ref, lk2_ref, swc_ref, o_ref, m_scr, l_scr, acc_scr = refs[2 * npg:]
    s_id = pl.program_id(1)
    qw = qw_ref[0]
    jp = qw.shape[1]

    @pl.when(s_id == 0)
    def _():
        m_scr[...] = jnp.full(m_scr.shape, -jnp.inf, F32)
        l_scr[...] = jnp.zeros(l_scr.shape, F32)
        acc_scr[...] = jnp.zeros(acc_scr.shape, F32)

    def same_head(nrows):
        row = lax.broadcasted_iota(jnp.int32, (nrows, jp), 0)
        col = lax.broadcasted_iota(jnp.int32, (nrows, jp), 1)
        return row, col, (_irem(row, nh) == _idiv(col, 2 * nq)) | (col >= ncol)

    def update(k, v, keep):
        s = jnp.where(keep, _dot(k, qw), -jnp.inf)
        m_prev = m_scr[...]
        m_new = jnp.maximum(m_prev, jnp.max(s, axis=0, keepdims=True))
        alpha = jnp.exp2(m_prev - m_new)
        p = jnp.exp2(s - m_new)
        l_scr[...] = alpha * l_scr[...] + jnp.sum(p, axis=0, keepdims=True)
        pv = lax.dot_general(v, p.astype(BF16), (((0,), (0,)), ((), ())), preferred_element_type=F32)
        acc_scr[...] = alpha * acc_scr[...] + pv
        m_scr[...] = m_new

    _, _, keep_page = same_head(kp_refs[0].shape[1])
    for j in range(npg):
        update(kp_refs[j][0].astype(BF16), vp_refs[j][0].astype(BF16), keep_page)

    @pl.when(s_id == pl.num_programs(1) - 1)
    def _():
        row, col, keep = same_head(kn_ref.shape[1])
        update(kn_ref[0], vn_ref[0], keep & ((_idiv(row, nh) <= _irem(col, nq)) | (col >= ncol)))
        lam = _lambda(lq1_ref[...], lk1_ref[...], lq2_ref[...], lk2_ref[...], lam_init)
        on = acc_scr[...] / l_scr[...]
        o = on - lam * pltpu.roll(on, jp - nq, axis=1)
        ms = jnp.mean(o * o, axis=0, keepdims=True)
        o_ref[0] = o * lax.rsqrt(ms + NORM_EPS) * swc_ref[...] * (1.0 - lam_init)


def _sattn_call(page_table, qw, cache_k, cache_v, k_new, v_new, lq1, lk1, lq2, lk2, subw_col, nh, nq, ncol,
                lam_init):
    nbatch, kd, jp = qw.shape
    prow = cache_k.shape[1]
    vd = cache_v.shape[2]
    hd = lq1.shape[1]
    npages = page_table.shape[1]
    npg = _tile(npages, 8)
    nk = k_new.shape[1]
    kern = functools.partial(_sattn_kernel, npg=npg, nh=nh, nq=nq, ncol=ncol, lam_init=lam_init)

    def page_spec(j, width):
        return pl.BlockSpec((1, prow, width), lambda b, s, pt: (pt[b, s * npg + j], 0, 0))

    vec = pl.BlockSpec((1, hd), lambda b, s, pt: (0, 0))
    grid_spec = pltpu.PrefetchScalarGridSpec(
        num_scalar_prefetch=1,
        grid=(nbatch, npages // npg),
        in_specs=([pl.BlockSpec((1, kd, jp), lambda b, s, pt: (b, 0, 0))]
                  + [page_spec(j, kd) for j in range(npg)] + [page_spec(j, vd) for j in range(npg)]
                  + [pl.BlockSpec((1, nk, kd), lambda b, s, pt: (b, 0, 0)),
                     pl.BlockSpec((1, nk, vd), lambda b, s, pt: (b, 0, 0)),
                     vec, vec, vec, vec,
                     pl.BlockSpec((vd, 1), lambda b, s, pt: (0, 0))]),
        out_specs=pl.BlockSpec((1, vd, jp), lambda b, s, pt: (b, 0, 0)),
        scratch_shapes=[pltpu.VMEM((1, jp), F32), pltpu.VMEM((1, jp), F32), pltpu.VMEM((vd, jp), F32)],
    )
    return pl.pallas_call(
        kern,
        grid_spec=grid_spec,
        out_shape=jax.ShapeDtypeStruct((nbatch, vd, jp), F32),
        compiler_params=_params("parallel", "arbitrary"),
        name="sample_attn",
    )(page_table, qw, *([cache_k] * npg), *([cache_v] * npg), k_new, v_new, lq1, lk1, lq2, lk2, subw_col)


def _to_groups(h, nbatch, g, pad_to):
    m, d = h.shape
    t = m // nbatch
    gc = d // g
    h4 = h.reshape(nbatch, t, g, gc)
    if pad_to > t:
        h4 = jnp.pad(h4, ((0, 0), (0, pad_to - t), (0, 0), (0, 0)))
    nc = pad_to // SSM_CHUNK
    h5 = h4.reshape(nbatch, nc, SSM_CHUNK, g, gc)
    return jnp.transpose(h5, (3, 0, 1, 2, 4)).reshape(g, nbatch * nc, SSM_CHUNK * gc)


def _from_groups(yg, nbatch, t):
    g, r, w = yg.shape
    nc = r // nbatch
    gc = w // SSM_CHUNK
    y5 = yg.reshape(g, nbatch, nc, SSM_CHUNK, gc)
    y4 = jnp.transpose(y5, (1, 2, 3, 0, 4)).reshape(nbatch, nc * SSM_CHUNK, g * gc)
    return y4[:, :t].reshape(nbatch * t, g * gc)


def _run_group(x3, mods, kvmod, s_re0, s_im0, attend, p, nb):
    nbatch, t, d = x3.shape
    m = nbatch * t
    x = x3.reshape(m, d)
    depth = p['norm_w'].shape[0]
    n_a = p['lam_re'].shape[0]
    g, n = p['lam_re'].shape[1:]
    t_pad = -(-t // SSM_CHUNK) * SSM_CHUNK
    nc = t_pad // SSM_CHUNK
    tv = SSM_CHUNK if nc > 1 else t
    new_re, new_im = [], []
    k = v = kb = vb = None
    for l in range(depth):
        sh1, sc1, g1, sh2, sc2, g2 = mods[l]
        nw1 = p['norm_w'][l, 0][None]
        if l < n_a:
            tabs = _ssm_tables(p['lam_re'][l], p['lam_im'][l], p['log_step'][l], p['b_re'][l], p['b_im'][l],
                               p['c_re'][l], p['c_im'][l], tv, nc)
            h = _norm_mod_call(x, nw1, sh1, sc1, nb)
            s0 = jnp.concatenate([jnp.transpose(s_re0[l], (1, 0, 2)), jnp.transpose(s_im0[l], (1, 0, 2))], -1)
            yg, sf = _ssm_call(_to_groups(h, nbatch, g, t_pad), *tabs, s0.astype(F32), nbatch)
            new_re.append(jnp.transpose(sf[..., :n], (1, 0, 2)))
            new_im.append(jnp.transpose(sf[..., n:], (1, 0, 2)))
            x = _glu_call(x, _from_groups(yg, nbatch, t), nw1, sh1, sc1, p['ssm_d'][l][None], p['glu_w'][l], g1, nb)
        else:
            j = l - n_a
            lam_init = 0.8 - 0.6 * math.exp(-0.3 * l)
            hd = p['q_norm_w'].shape[1]
            q = _qproj_call(x, nw1, sh1, sc1, p['attn_wq'][j], p['q_norm_w'][j][None], nb,
                            hd ** -0.5 * math.log2(math.e))
            o = attend(q, kb, vb, p['lambda_q1'][j][None], p['lambda_k1'][j][None], p['lambda_q2'][j][None],
                       p['lambda_k2'][j][None], p['subln_w'][j][None], lam_init)
            x = _oproj_call(o, p['attn_wo'][j], x, g1, nb)
        x = _mlp_call(x, p['norm_w'][l, 1][None], sh2, sc2, g2, p['mlp_up'][l], p['mlp_down'][l], nb)
        if l == n_a - 1:
            k, v, kb, vb = _kvproj_call(x, p['kv_norm_w'][None], kvmod[0], kvmod[1], p['kv_w'], p['k_norm_w'][None],
                                        nb)
    return x.reshape(nbatch, t, d), jnp.stack(new_re), jnp.stack(new_im), k, v


def kernel(x_prompt, x_sample, state_ssm_re, state_ssm_im, cache_k, cache_v, page_table, c_prompt, c_sample, ada_w, ada_b, norm_w, mlp_up, mlp_down, ssm_lambda_re, ssm_lambda_im, ssm_log_step, ssm_b_re, ssm_b_im, ssm_c_re, ssm_c_im, ssm_d, glu_w, kv_ada_w, kv_ada_b, kv_norm_w, kv_w, k_norm_w, attn_wq, q_norm_w, lambda_q1, lambda_k1, lambda_q2, lambda_k2, subln_w, attn_wo):
    bp, tp, d = x_prompt.shape
    bs, ts, _ = x_sample.shape
    depth = ada_w.shape[0]
    n_a = ssm_lambda_re.shape[0]
    nh, vd = cache_v.shape[2], cache_v.shape[3]
    hd = q_norm_w.shape[1]

    p = {
        'norm_w': norm_w, 'mlp_up': mlp_up.astype(BF16), 'mlp_down': mlp_down.astype(BF16),
        'lam_re': ssm_lambda_re, 'lam_im': ssm_lambda_im, 'log_step': ssm_log_step,
        'b_re': ssm_b_re.astype(F32), 'b_im': ssm_b_im.astype(F32),
        'c_re': ssm_c_re.astype(F32), 'c_im': ssm_c_im.astype(F32),
        'ssm_d': ssm_d, 'glu_w': glu_w.astype(BF16), 'kv_norm_w': kv_norm_w, 'kv_w': kv_w.astype(BF16),
        'k_norm_w': k_norm_w, 'attn_wq': attn_wq.astype(BF16), 'q_norm_w': q_norm_w,
        'lambda_q1': lambda_q1, 'lambda_k1': lambda_k1, 'lambda_q2': lambda_q2, 'lambda_k2': lambda_k2,
        'subln_w': subln_w, 'attn_wo': attn_wo.astype(BF16),
    }

    nc_rows = bp + bs
    c_all = jnp.concatenate([c_prompt, c_sample], axis=0)
    c_all = jnp.pad(c_all, ((0, -nc_rows % 16), (0, 0))).astype(BF16)
    mod_all = _ada(c_all, ada_w, ada_b[:, None, :])
    kvmod_all = _ada(c_all, kv_ada_w[None], kv_ada_b[None, None, :])[0]

    def split_mods(lo, hi, rep):
        def shape(a):
            return a[:, None, :] if rep == 1 else jnp.repeat(a, rep, axis=0)[None]
        mods = [[shape(mod_all[l, lo:hi, i * d:(i + 1) * d]) for i in range(6)] for l in range(depth)]
        kvmod = [shape(kvmod_all[lo:hi, i * d:(i + 1) * d]) for i in range(2)]
        return mods, kvmod

    mods_p, kvmod_p = split_mods(0, bp, 1)
    zeros = jnp.zeros((n_a, bp) + ssm_lambda_re.shape[1:], F32)

    def attend_prompt(q, kb, vb, lq1, lk1, lq2, lk2, subw, lam_init):
        return _pattn_call(q, kb, vb, lq1, lk1, lq2, lk2, subw, bp, lam_init)

    y_p, re_p, im_p, k_p, v_p = _run_group(x_prompt, mods_p, kvmod_p, zeros, zeros, attend_prompt, p, bp)

    mods_s, kvmod_s = split_mods(bp, bp + bs, ts)
    page = cache_k.shape[1]
    ck = cache_k.reshape(cache_k.shape[0], page * nh, 2 * hd)
    cv = cache_v.reshape(cache_v.shape[0], page * nh, vd)
    eye2 = jnp.eye(2, dtype=BF16)
    ncol = nh * 2 * ts
    jp = -(-ncol // 128) * 128

    def attend_sample(q, kb, vb, lq1, lk1, lq2, lk2, subw, lam_init):
        q5 = q.reshape(bs, ts, nh, 2, hd)
        qw = jnp.einsum('bihmd,mM->bmdhMi', q5, eye2).reshape(bs, 2 * hd, ncol)
        qw = jnp.pad(qw, ((0, 0), (0, 0), (0, jp - ncol)))
        k_new = kb.reshape(bs, ts * nh, 2 * hd)
        v_new = vb.reshape(bs, ts * nh, vd)
        ot = _sattn_call(page_table, qw, ck, cv, k_new, v_new, lq1, lk1, lq2, lk2, subw.reshape(vd, 1),
                         nh, ts, ncol, lam_init)
        o = ot[:, :, :ncol].reshape(bs, vd, nh, 2, ts)[:, :, :, 0, :]
        return jnp.transpose(o, (0, 3, 2, 1)).reshape(bs * ts, d).astype(BF16)

    y_s, re_s, im_s, k_s, v_s = _run_group(x_sample, mods_s, kvmod_s, state_ssm_re, state_ssm_im, attend_sample,
                                           p, 1)

    return (y_p, y_s, re_p, im_p,
            k_p.reshape(bp, tp, nh, 2 * hd), v_p.reshape(bp, tp, nh, vd),
            re_s, im_s,
            k_s.reshape(bs, ts, nh, 2 * hd), v_s.reshape(bs, ts, nh, vd))
```

```python
import functools
import math

import jax
import jax.numpy as jnp
from jax import lax
from jax.experimental import pallas as pl
from jax.experimental.pallas import tpu as pltpu

F32 = jnp.float32
BF16 = jnp.bfloat16
HI = lax.Precision.HIGHEST

NORM_EPS = 1e-6
SSM_CHUNK = 16
SLAB = 256
S5_BLOCK_CHUNKS = 256
MIB = 1024 * 1024
VMEM_LIMIT_BYTES = 52 * MIB
ATTN_TILE = 512


def _params(*sem):
    return pltpu.CompilerParams(dimension_semantics=sem, vmem_limit_bytes=VMEM_LIMIT_BYTES)


def _tile(dim, pref):
    t = min(dim, pref)
    assert dim % t == 0, (dim, pref)
    return t


def _dot(a, b):
    return jnp.dot(a, b, preferred_element_type=F32)


def _dot_nt(a, b):
    return lax.dot_general(a, b, (((1,), (1,)), ((), ())), preferred_element_type=F32)


def _rms(x, w):
    return x * lax.rsqrt(jnp.mean(x * x, axis=-1, keepdims=True) + NORM_EPS) * w


def _norm_mod(x, nw, sh, sc):
    return _rms(x, nw) * (1.0 + sc) + sh


def _chunk_rms(y, w, width):
    parts = []
    for c in range(y.shape[-1] // width):
        parts.append(_rms(y[:, c * width:(c + 1) * width], w))
    return parts[0] if len(parts) == 1 else jnp.concatenate(parts, axis=-1)


def _ada_kernel(c_ref, w_ref, b_ref, o_ref):
    o_ref[0] = _dot(c_ref[...], w_ref[0].astype(BF16)) + b_ref[0]


def _ada(c, w, b):
    nl, d, n = w.shape
    r = c.shape[0]
    tn = _tile(n, 512)
    return pl.pallas_call(
        _ada_kernel,
        grid=(nl, n // tn),
        in_specs=[pl.BlockSpec((r, d), lambda l, j: (0, 0)),
                  pl.BlockSpec((1, d, tn), lambda l, j: (l, 0, j)),
                  pl.BlockSpec((1, 1, tn), lambda l, j: (l, 0, j))],
        out_specs=pl.BlockSpec((1, r, tn), lambda l, j: (l, 0, j)),
        out_shape=jax.ShapeDtypeStruct((nl, r, n), F32),
        compiler_params=_params("parallel", "parallel"),
        name="ada",
    )(c, w, b)


class _Rows:
    def __init__(self, m, nb, tm_pref):
        self.m = m
        self.nb = nb
        rows_per_nb = m // nb
        self.tm = _tile(rows_per_nb, tm_pref)
        self.tiles_per_nb = rows_per_nb // self.tm
        self.nt = m // self.tm

    def mod_spec(self, r, width, col=None):
        tpn = self.tiles_per_nb
        if col is None:
            return pl.BlockSpec((1, r, width), lambda i, j: (i // tpn, 0, 0))
        return pl.BlockSpec((1, r, width), lambda i, j: (i // tpn, 0, j))


def _norm_mod_kernel(x_ref, nw_ref, sh_ref, sc_ref, o_ref):
    o_ref[...] = _norm_mod(x_ref[...], nw_ref[...], sh_ref[0], sc_ref[0]).astype(BF16)


def _norm_mod_call(x, nw, sh, sc, nb):
    m, d = x.shape
    rows = _Rows(m, nb, 512)
    tm, r = rows.tm, sh.shape[1]
    tpn = rows.tiles_per_nb
    return pl.pallas_call(
        _norm_mod_kernel,
        grid=(rows.nt,),
        in_specs=[pl.BlockSpec((tm, d), lambda i: (i, 0)),
                  pl.BlockSpec((1, d), lambda i: (0, 0)),
                  pl.BlockSpec((1, r, d), lambda i: (i // tpn, 0, 0)),
                  pl.BlockSpec((1, r, d), lambda i: (i // tpn, 0, 0))],
        out_specs=pl.BlockSpec((tm, d), lambda i: (i, 0)),
        out_shape=jax.ShapeDtypeStruct((m, d), BF16),
        compiler_params=_params("parallel"),
        name="norm_mod",
    )(x, nw, sh, sc)


def _cmul(a_ref_val, s, half):
    return a_ref_val[0:1, :] * s + a_ref_val[1:2, :] * pltpu.roll(s, half, axis=1)


def _class_perm(tr, cr, ncls, transpose):
    a = lax.broadcasted_iota(jnp.int32, (tr, tr), 1 if transpose else 0)
    b = lax.broadcasted_iota(jnp.int32, (tr, tr), 0 if transpose else 1)
    return (b == _irem(a, cr) * ncls + _idiv(a, cr)).astype(BF16)


def _ssm_kernel(h_ref, kbd_ref, wst_ref, wout_ref, apow_ref, s0_ref, y_ref, sf_ref, u_scr, yc_scr, sin_scr,
                *, ncls, nb, nc, half, unit_lanes, cr):
    r = nb * nc
    n2 = 2 * half
    nsteps = nc.bit_length() - 1
    nunit = SLAB // unit_lanes
    gpu = SLAB // n2
    tr = ncls * cr
    ntile = r // cr
    row = lax.broadcasted_iota(jnp.int32, (r, n2), 0)
    chunk = row & (nc - 1)
    wrow = _idiv(lax.broadcasted_iota(jnp.int32, (SLAB, SLAB), 0), unit_lanes)
    wcol = _idiv(lax.broadcasted_iota(jnp.int32, (SLAB, SLAB), 1), unit_lanes)

    perm = _class_perm(tr, cr, ncls, False)
    for i in range(ntile):
        uc = _dot(perm, h_ref[i * tr:(i + 1) * tr, :]).astype(BF16)
        for k in range(ncls):
            u_scr[k, i * cr:(i + 1) * cr, :] = uc[k * cr:(k + 1) * cr, :]

    def carry_states(p, apow, s0):
        s0_rows = jnp.zeros((r, n2), F32)
        for b in range(nb):
            s0_rows = jnp.where(row == b * nc, s0[b:b + 1, :], s0_rows)
        p = p + _cmul(apow[0], s0_rows, half)
        for k in range(nsteps):
            d = 1 << k
            shifted = jnp.where(chunk >= d, pltpu.roll(p, d, axis=0), 0.0)
            p = p + _cmul(apow[k], shifted, half)
        s_in = s0_rows if nc == 1 else jnp.where(chunk == 0, s0_rows, pltpu.roll(p, 1, axis=0))
        return s_in, p

    def unit_body(ui, carry):
        sl = None
        for k in range(ncls):
            t = _dot(u_scr[k], jnp.where(wrow == ui, wst_ref[0, k], jnp.zeros((), BF16)))
            sl = t if sl is None else sl + t
        parts = []
        for hf in range(gpu):
            g = ui * gpu + hf
            s_in, p = carry_states(sl[:, hf * n2:(hf + 1) * n2], apow_ref[g], s0_ref[g, 0])
            parts.append(s_in.astype(BF16))
            for b in range(nb):
                last = b * nc + nc - 1
                sf_ref[g, 0, b:b + 1, :] = p[last:last + 1, :]
        sin_scr[ui] = parts[0] if gpu == 1 else jnp.concatenate(parts, axis=1)
        return carry

    lax.fori_loop(0, nunit, unit_body, 0)

    for t in range(ncls):
        acc = None
        for k in range(t + 1):
            d = _dot(u_scr[k], kbd_ref[0, t - k])
            acc = d if acc is None else acc + d
        for ui in range(nunit):
            acc = acc + _dot(sin_scr[ui], jnp.where(wcol == ui, wout_ref[0, t], jnp.zeros((), BF16)))
        yc_scr[t] = acc

    perm_t = _class_perm(tr, cr, ncls, True)
    for i in range(ntile):
        yc = jnp.concatenate([yc_scr[k, i * cr:(i + 1) * cr, :] for k in range(ncls)], axis=0)
        hi = yc.astype(BF16)
        lo = (yc - hi.astype(F32)).astype(BF16)
        y_ref[i * tr:(i + 1) * tr, :] = _dot(perm_t, hi) + _dot(perm_t, lo)


def _ssm_call(h, kbd, wst, wout, apow, s0, nbatch):
    m, d = h.shape
    ncls = kbd.shape[1]
    t = m // nbatch
    nc = t // ncls
    assert nc & (nc - 1) == 0
    g, ns, _, n2 = apow.shape
    nslab = d // SLAB
    gps = g // nslab
    unit_lanes = (SLAB // n2) * (SLAB // gps)
    bpb = max(b for b in range(1, nbatch + 1) if nbatch % b == 0 and b * nc <= S5_BLOCK_CHUNKS)
    nblk = nbatch // bpb
    r = bpb * nc
    cr = min(r, 32)
    kern = functools.partial(_ssm_kernel, ncls=ncls, nb=bpb, nc=nc, half=n2 // 2, unit_lanes=unit_lanes, cr=cr)
    tab = pl.BlockSpec((1, ncls, SLAB, SLAB), lambda i, j: (i, 0, 0, 0))
    st = pl.BlockSpec((gps, 1, bpb, n2), lambda i, j: (i, j, 0, 0))
    y, sf = pl.pallas_call(
        kern,
        grid=(nslab, nblk),
        in_specs=[pl.BlockSpec((bpb * t, SLAB), lambda i, j: (j, i)),
                  tab, tab, tab,
                  pl.BlockSpec((gps, ns, 2, n2), lambda i, j: (i, 0, 0, 0)),
                  st],
        out_specs=[pl.BlockSpec((bpb * t, SLAB), lambda i, j: (j, i)), st],
        out_shape=[jax.ShapeDtypeStruct((m, d), F32),
                   jax.ShapeDtypeStruct((g, nblk, bpb, n2), F32)],
        scratch_shapes=[pltpu.VMEM((ncls, r, SLAB), BF16), pltpu.VMEM((ncls, r, SLAB), F32),
                        pltpu.VMEM((SLAB // unit_lanes, r, SLAB), BF16)],
        compiler_params=_params("parallel", "parallel"),
        name="ssm",
    )(h, kbd, wst, wout, apow, s0.reshape(g, nblk, bpb, n2))
    return y, sf.reshape(g, nbatch, n2)


def _ssm_tables(lam_re, lam_im, log_step, b_re, b_im, c_re, c_im, tv, nc):
    L = tv
    g, n = lam_re.shape
    gc = b_re.shape[2]
    lr, li = lam_re.astype(F32), lam_im.astype(F32)
    dt = jnp.exp(log_step.astype(F32))[:, None]
    mag = jnp.exp(lr * dt)
    ar, ai = mag * jnp.cos(li * dt), mag * jnp.sin(li * dt)
    den = lr * lr + li * li
    xr = ar - 1.0
    f_re = (xr * lr + ai * li) / den
    f_im = (ai * lr - xr * li) / den
    bb_re = f_re[..., None] * b_re - f_im[..., None] * b_im
    bb_im = f_re[..., None] * b_im + f_im[..., None] * b_re
    j = jnp.arange(L + 1, dtype=F32)[:, None, None]
    pmag = jnp.exp(j * (lr * dt))
    pw_re, pw_im = pmag * jnp.cos(j * (li * dt)), pmag * jnp.sin(j * (li * dt))
    pb_re = pw_re[..., None] * bb_re - pw_im[..., None] * bb_im
    pb_im = pw_re[..., None] * bb_im + pw_im[..., None] * bb_re
    kj = (jnp.einsum('gcn,jgnd->jgcd', c_re, pb_re[:L], precision=HI)
          - jnp.einsum('gcn,jgnd->jgcd', c_im, pb_im[:L], precision=HI))
    nslab = g * gc // SLAB
    gps = g // nslab
    n2 = 2 * n
    gpu = SLAB // n2
    same = jnp.eye(gps, dtype=F32)
    par = (jnp.arange(gps)[:, None] % gpu == jnp.arange(gpu)[None, :]).astype(F32)
    kbd = jnp.einsum('jsgcd,hg->sjhdgc', kj.reshape(L, nslab, gps, gc, gc), same).reshape(nslab, L, SLAB, SLAB)
    ws = jnp.concatenate([pb_re[:L][::-1], pb_im[:L][::-1]], axis=2)
    wst = jnp.einsum('ksgnd,gh->skgdhn', ws.reshape(L, nslab, gps, n2, gc), par).reshape(nslab, L, SLAB, SLAB)
    cp_re = c_re[None] * pw_re[1:, :, None, :] - c_im[None] * pw_im[1:, :, None, :]
    cp_im = c_re[None] * pw_im[1:, :, None, :] + c_im[None] * pw_re[1:, :, None, :]
    wo = jnp.concatenate([cp_re, -cp_im], axis=3)
    wout = jnp.einsum('tsgcn,gh->sthngc', wo.reshape(L, nslab, gps, gc, n2), par).reshape(nslab, L, SLAB, SLAB)
    are, aim = pw_re[tv], pw_im[tv]
    steps = []
    for _ in range(max(nc.bit_length() - 1, 1)):
        steps.append(jnp.stack([jnp.concatenate([are, are], -1), jnp.concatenate([-aim, aim], -1)], axis=1))
        are, aim = are * are - aim * aim, 2.0 * are * aim
    apow = jnp.stack(steps, axis=1)
    return kbd.astype(BF16), wst.astype(BF16), wout.astype(BF16), apow


def _glu_kernel(x_ref, y_ref, nw_ref, sh_ref, sc_ref, d_ref, wv_ref, wg_ref, xr_ref, g_ref, o_ref, a_scr):
    @pl.when(pl.program_id(1) == 0)
    def _():
        h = _norm_mod(x_ref[...], nw_ref[...], sh_ref[0], sc_ref[0])
        a_scr[...] = jax.nn.gelu(y_ref[...] + d_ref[...] * h).astype(BF16)

    a = a_scr[...]
    val = _dot(a, wv_ref[...])
    gate = _dot(a, wg_ref[...])
    o_ref[...] = xr_ref[...] + g_ref[0] * (val / (1.0 + jnp.exp(-gate)))


def _glu_call(x, y, nw, sh, sc, dskip, w, g1, nb):
    m, d = x.shape
    rows = _Rows(m, nb, 512)
    tm, r = rows.tm, sh.shape[1]
    tn = _tile(d, 512)
    nn = d // tn
    return pl.pallas_call(
        _glu_kernel,
        grid=(rows.nt, nn),
        in_specs=[pl.BlockSpec((tm, d), lambda i, j: (i, 0)),
                  pl.BlockSpec((tm, d), lambda i, j: (i, 0)),
                  pl.BlockSpec((1, d), lambda i, j: (0, 0)),
                  rows.mod_spec(r, d), rows.mod_spec(r, d),
                  pl.BlockSpec((1, d), lambda i, j: (0, 0)),
                  pl.BlockSpec((d, tn), lambda i, j: (0, j)),
                  pl.BlockSpec((d, tn), lambda i, j: (0, j + nn)),
                  pl.BlockSpec((tm, tn), lambda i, j: (i, j)),
                  rows.mod_spec(r, tn, col=True)],
        out_specs=pl.BlockSpec((tm, tn), lambda i, j: (i, j)),
        out_shape=jax.ShapeDtypeStruct((m, d), F32),
        scratch_shapes=[pltpu.VMEM((tm, d), BF16)],
        compiler_params=_params("parallel", "arbitrary"),
        name="glu",
    )(x, y, nw, sh, sc, dskip, w, w, x, g1)


def _mlp_kernel(x_ref, nw_ref, sh_ref, sc_ref, g_ref, up_ref, dn_ref, o_ref, h_scr, *, nf):
    f = pl.program_id(1)

    @pl.when(f == 0)
    def _():
        h_scr[...] = _norm_mod(x_ref[...], nw_ref[...], sh_ref[0], sc_ref[0]).astype(BF16)

    a = jnp.maximum(_dot(h_scr[...], up_ref[...]), 0.0)
    part = _dot((a * a).astype(BF16), dn_ref[...])

    @pl.when(f == 0)
    def _():
        o_ref[...] = part

    @pl.when(f > 0)
    def _():
        o_ref[...] += part

    @pl.when(f == nf - 1)
    def _():
        o_ref[...] = x_ref[...] + g_ref[0] * o_ref[...]


def _mlp_call(x, nw, sh, sc, g2, up, dn, nb):
    m, d = x.shape
    ff = up.shape[1]
    rows = _Rows(m, nb, 512)
    tm, r = rows.tm, sh.shape[1]
    tf = _tile(ff, 1024)
    nf = ff // tf
    return pl.pallas_call(
        functools.partial(_mlp_kernel, nf=nf),
        grid=(rows.nt, nf),
        in_specs=[pl.BlockSpec((tm, d), lambda i, j: (i, 0)),
                  pl.BlockSpec((1, d), lambda i, j: (0, 0)),
                  rows.mod_spec(r, d), rows.mod_spec(r, d), rows.mod_spec(r, d),
                  pl.BlockSpec((d, tf), lambda i, j: (0, j)),
                  pl.BlockSpec((tf, d), lambda i, j: (j, 0))],
        out_specs=pl.BlockSpec((tm, d), lambda i, j: (i, 0)),
        out_shape=jax.ShapeDtypeStruct((m, d), F32),
        scratch_shapes=[pltpu.VMEM((tm, d), BF16)],
        compiler_params=_params("parallel", "arbitrary"),
        name="mlp",
    )(x, nw, sh, sc, g2, up, dn)


def _mod_rows(ref, r0, ts):
    return ref[0] if ref.shape[1] == 1 else ref[0, r0:r0 + ts, :]


def _qproj_kernel(x_ref, nw_ref, sh_ref, sc_ref, w_ref, cw_ref, q_ref, *, hd, ts, qscale):
    cw = cw_ref[...] * qscale
    for r0 in range(0, x_ref.shape[0], ts):
        h = _norm_mod(x_ref[r0:r0 + ts, :], nw_ref[...], _mod_rows(sh_ref, r0, ts), _mod_rows(sc_ref, r0, ts))
        q = _dot(h.astype(BF16), w_ref[...])
        for c in range(q.shape[1] // hd):
            q_ref[r0:r0 + ts, c * hd:(c + 1) * hd] = _rms(q[:, c * hd:(c + 1) * hd], cw).astype(BF16)


def _qproj_call(x, nw, sh, sc, wq, qnw, nb, qscale):
    m, d = x.shape
    n = wq.shape[1]
    hd = qnw.shape[1]
    rows = _Rows(m, nb, 512)
    tm, r = rows.tm, sh.shape[1]
    tpn = rows.tiles_per_nb
    mod = pl.BlockSpec((1, r, d), lambda i: (i // tpn, 0, 0))
    return pl.pallas_call(
        functools.partial(_qproj_kernel, hd=hd, ts=min(tm, 256), qscale=qscale),
        grid=(rows.nt,),
        in_specs=[pl.BlockSpec((tm, d), lambda i: (i, 0)),
                  pl.BlockSpec((1, d), lambda i: (0, 0)),
                  mod, mod,
                  pl.BlockSpec((d, n), lambda i: (0, 0)),
                  pl.BlockSpec((1, hd), lambda i: (0, 0))],
        out_specs=pl.BlockSpec((tm, n), lambda i: (i, 0)),
        out_shape=jax.ShapeDtypeStruct((m, n), BF16),
        compiler_params=_params("parallel"),
        name="q_proj",
    )(x, nw, sh, sc, wq, qnw)


def _kvproj_kernel(x_ref, nw_ref, sh_ref, sc_ref, wk_ref, wv_ref, cw_ref, k_ref, v_ref, kb_ref, vb_ref, h_scr,
                   *, hd):
    @pl.when(pl.program_id(1) == 0)
    def _():
        h_scr[...] = _norm_mod(x_ref[...], nw_ref[...], sh_ref[0], sc_ref[0]).astype(BF16)

    h = h_scr[...]
    k = _chunk_rms(_dot(h, wk_ref[...]), cw_ref[...], hd)
    v = _dot(h, wv_ref[...])
    k_ref[...] = k
    v_ref[...] = v
    kb_ref[...] = k.astype(BF16)
    vb_ref[...] = v.astype(BF16)


def _kvproj_call(x, nw, sh, sc, w, knw, nb):
    m, d = x.shape
    nk = w.shape[1] // 2
    hd = knw.shape[1]
    rows = _Rows(m, nb, 512)
    tm, r = rows.tm, sh.shape[1]
    tn = _tile(nk, 512)
    nn = nk // tn
    out_spec = pl.BlockSpec((tm, tn), lambda i, j: (i, j))
    return pl.pallas_call(
        functools.partial(_kvproj_kernel, hd=hd),
        grid=(rows.nt, nn),
        in_specs=[pl.BlockSpec((tm, d), lambda i, j: (i, 0)),
                  pl.BlockSpec((1, d), lambda i, j: (0, 0)),
                  rows.mod_spec(r, d), rows.mod_spec(r, d),
                  pl.BlockSpec((d, tn), lambda i, j: (0, j)),
                  pl.BlockSpec((d, tn), lambda i, j: (0, j + nn)),
                  pl.BlockSpec((1, hd), lambda i, j: (0, 0))],
        out_specs=[out_spec, out_spec, out_spec, out_spec],
        out_shape=[jax.ShapeDtypeStruct((m, nk), F32), jax.ShapeDtypeStruct((m, nk), F32),
                   jax.ShapeDtypeStruct((m, nk), BF16), jax.ShapeDtypeStruct((m, nk), BF16)],
        scratch_shapes=[pltpu.VMEM((tm, d), BF16)],
        compiler_params=_params("parallel", "arbitrary"),
        name="kv_proj",
    )(x, nw, sh, sc, w, w, knw)


def _oproj_kernel(o_ref, w_ref, x_ref, g_ref, y_ref):
    y_ref[...] = x_ref[...] + g_ref[0] * _dot(o_ref[...], w_ref[...])


def _oproj_call(o, wo, x, g1, nb):
    m, d = x.shape
    rows = _Rows(m, nb, 512)
    tm, r = rows.tm, g1.shape[1]
    tpn = rows.tiles_per_nb
    return pl.pallas_call(
        _oproj_kernel,
        grid=(rows.nt,),
        in_specs=[pl.BlockSpec((tm, o.shape[1]), lambda i: (i, 0)),
                  pl.BlockSpec((o.shape[1], d), lambda i: (0, 0)),
                  pl.BlockSpec((tm, d), lambda i: (i, 0)),
                  pl.BlockSpec((1, r, d), lambda i: (i // tpn, 0, 0))],
        out_specs=pl.BlockSpec((tm, d), lambda i: (i, 0)),
        out_shape=jax.ShapeDtypeStruct((m, d), F32),
        compiler_params=_params("parallel"),
        name="o_proj",
    )(o, wo, x, g1)


def _lambda(lq1, lk1, lq2, lk2, lam_init):
    return (jnp.exp(jnp.sum(lq1 * lk1, axis=-1, keepdims=True))
            - jnp.exp(jnp.sum(lq2 * lk2, axis=-1, keepdims=True)) + lam_init)


def _pattn_kernel(q_ref, k_ref, v_ref, lq1_ref, lk1_ref, lq2_ref, lk2_ref, sw_ref, o_ref, m_scr, l_scr, acc_scr,
                  *, hd, tq, lam_init):
    qi = pl.program_id(2)
    m_scr[...] = jnp.full(m_scr.shape, -jnp.inf, F32)
    l_scr[...] = jnp.zeros(l_scr.shape, F32)
    acc_scr[...] = jnp.zeros(acc_scr.shape, F32)

    def block(kstart, masked):
        k = k_ref[pl.ds(kstart, tq), :]
        v = v_ref[pl.ds(kstart, tq), :]
        old = [(m_scr[mp], l_scr[mp], acc_scr[mp]) for mp in range(2)]
        s = [_dot_nt(q_ref[:, mp * hd:(mp + 1) * hd], k[:, mp * hd:(mp + 1) * hd]) for mp in range(2)]
        if masked:
            rowi = lax.broadcasted_iota(jnp.int32, (tq, tq), 0)
            coli = lax.broadcasted_iota(jnp.int32, (tq, tq), 1)
            s = [jnp.where(coli <= rowi, x, -jnp.inf) for x in s]
        new = []
        for mp in range(2):
            m_prev, l_prev, acc_prev = old[mp]
            m_new = jnp.maximum(m_prev, jnp.max(s[mp], axis=-1, keepdims=True))
            alpha = jnp.exp2(m_prev - m_new)
            p = jnp.exp2(s[mp] - m_new)
            new.append((m_new, alpha * l_prev + jnp.sum(p, axis=-1, keepdims=True),
                        alpha * acc_prev + _dot(p.astype(BF16), v)))
        for mp in range(2):
            m_scr[mp], l_scr[mp], acc_scr[mp] = new[mp]

    def body(j, carry):
        block(pl.multiple_of(j * tq, tq), False)
        return carry

    lax.fori_loop(0, qi, body, 0)
    block(pl.multiple_of(qi * tq, tq), True)
    lam = _lambda(lq1_ref[...], lk1_ref[...], lq2_ref[...], lk2_ref[...], lam_init)
    o = acc_scr[0] / l_scr[0] - lam * (acc_scr[1] / l_scr[1])
    o_ref[...] = (_rms(o, sw_ref[...]) * (1.0 - lam_init)).astype(BF16)


def _pattn_call(q, kb, vb, lq1, lk1, lq2, lk2, subw, nbatch, lam_init):
    m, d = q.shape
    vd = subw.shape[1]
    hd = lq1.shape[1]
    nh = d // vd
    t = m // nbatch
    tq = _tile(t, ATTN_TILE)
    nq = t // tq
    kern = functools.partial(_pattn_kernel, hd=hd, tq=tq, lam_init=lam_init)
    vec = pl.BlockSpec((1, hd), lambda b, h, i: (0, 0))
    return pl.pallas_call(
        kern,
        grid=(nbatch, nh, nq),
        in_specs=[pl.BlockSpec((tq, vd), lambda b, h, i: (b * nq + i, h)),
                  pl.BlockSpec((t, vd), lambda b, h, i: (b, h)),
                  pl.BlockSpec((t, vd), lambda b, h, i: (b, h)),
                  vec, vec, vec, vec,
                  pl.BlockSpec((1, vd), lambda b, h, i: (0, 0))],
        out_specs=pl.BlockSpec((tq, vd), lambda b, h, i: (b * nq + i, h)),
        out_shape=jax.ShapeDtypeStruct((m, d), BF16),
        scratch_shapes=[pltpu.VMEM((2, tq, 1), F32), pltpu.VMEM((2, tq, 1), F32), pltpu.VMEM((2, tq, vd), F32)],
        compiler_params=_params("parallel", "parallel", "parallel"),
        name="prompt_attn",
    )(q, kb, vb, lq1, lk1, lq2, lk2, subw)


def _idiv(x, n):
    if n & (n - 1) == 0:
        return lax.shift_right_logical(x, jnp.full_like(x, n.bit_length() - 1))
    return lax.div(x, jnp.full_like(x, n))


def _irem(x, n):
    if n & (n - 1) == 0:
        return x & (n - 1)
    return lax.rem(x, jnp.full_like(x, n))


def _sattn_kernel(pt_ref, qw_ref, *refs, npg, nh, nq, ncol, lam_init):
    kp_refs = refs[:npg]
    vp_refs = refs[npg:2 * npg]
    kn_ref, vn_ref, lq1_ref, lk1_ref, lq2_ref, lk2_ref, swc_ref, o_ref, m_scr, l_scr, acc_scr = refs[2 * npg:]
    s_id = pl.program_id(1)
    qw = qw_ref[0]
    jp = qw.shape[1]

    @pl.when(s_id == 0)
    def _():
        m_scr[...] = jnp.full(m_scr.shape, -jnp.inf, F32)
        l_scr[...] = jnp.zeros(l_scr.shape, F32)
        acc_scr[...] = jnp.zeros(acc_scr.shape, F32)

    def same_head(nrows):
        row = lax.broadcasted_iota(jnp.int32, (nrows, jp), 0)
        col = lax.broadcasted_iota(jnp.int32, (nrows, jp), 1)
        return row, col, (_irem(row, nh) == _idiv(col, 2 * nq)) | (col >= ncol)

    def update(k, v, keep):
        s = jnp.where(keep, _dot(k, qw), -jnp.inf)
        m_prev = m_scr[...]
        m_new = jnp.maximum(m_prev, jnp.max(s, axis=0, keepdims=True))
        alpha = jnp.exp2(m_prev - m_new)
        p = jnp.exp2(s - m_new)
        l_scr[...] = alpha * l_scr[...] + jnp.sum(p, axis=0, keepdims=True)
        pv = lax.dot_general(v, p.astype(BF16), (((0,), (0,)), ((), ())), preferred_element_type=F32)
        acc_scr[...] = alpha * acc_scr[...] + pv
        m_scr[...] = m_new

    _, _, keep_page = same_head(kp_refs[0].shape[1])
    for j in range(npg):
        update(kp_refs[j][0].astype(BF16), vp_refs[j][0].astype(BF16), keep_page)

    @pl.when(s_id == pl.num_programs(1) - 1)
    def _():
        row, col, keep = same_head(kn_ref.shape[1])
        update(kn_ref[0], vn_ref[0], keep & ((_idiv(row, nh) <= _irem(col, nq)) | (col >= ncol)))
        lam = _lambda(lq1_ref[...], lk1_ref[...], lq2_ref[...], lk2_ref[...], lam_init)
        on = acc_scr[...] / l_scr[...]
        o = on - lam * pltpu.roll(on, jp - nq, axis=1)
        ms = jnp.mean(o * o, axis=0, keepdims=True)
        o_ref[0] = o * lax.rsqrt(ms + NORM_EPS) * swc_ref[...] * (1.0 - lam_init)


def _sattn_call(page_table, qw, cache_k, cache_v, k_new, v_new, lq1, lk1, lq2, lk2, subw_col, nh, nq, ncol,
                lam_init):
    nbatch, kd, jp = qw.shape
    prow = cache_k.shape[1]
    vd = cache_v.shape[2]
    hd = lq1.shape[1]
    npages = page_table.shape[1]
    npg = _tile(npages, 8)
    nk = k_new.shape[1]
    kern = functools.partial(_sattn_kernel, npg=npg, nh=nh, nq=nq, ncol=ncol, lam_init=lam_init)

    def page_spec(j, width):
        return pl.BlockSpec((1, prow, width), lambda b, s, pt: (pt[b, s * npg + j], 0, 0))

    vec = pl.BlockSpec((1, hd), lambda b, s, pt: (0, 0))
    grid_spec = pltpu.PrefetchScalarGridSpec(
        num_scalar_prefetch=1,
        grid=(nbatch, npages // npg),
        in_specs=([pl.BlockSpec((1, kd, jp), lambda b, s, pt: (b, 0, 0))]
                  + [page_spec(j, kd) for j in range(npg)] + [page_spec(j, vd) for j in range(npg)]
                  + [pl.BlockSpec((1, nk, kd), lambda b, s, pt: (b, 0, 0)),
                     pl.BlockSpec((1, nk, vd), lambda b, s, pt: (b, 0, 0)),
                     vec, vec, vec, vec,
                     pl.BlockSpec((vd, 1), lambda b, s, pt: (0, 0))]),
        out_specs=pl.BlockSpec((1, vd, jp), lambda b, s, pt: (b, 0, 0)),
        scratch_shapes=[pltpu.VMEM((1, jp), F32), pltpu.VMEM((1, jp), F32), pltpu.VMEM((vd, jp), F32)],
    )
    return pl.pallas_call(
        kern,
        grid_spec=grid_spec,
        out_shape=jax.ShapeDtypeStruct((nbatch, vd, jp), F32),
        compiler_params=_params("parallel", "arbitrary"),
        name="sample_attn",
    )(page_table, qw, *([cache_k] * npg), *([cache_v] * npg), k_new, v_new, lq1, lk1, lq2, lk2, subw_col)


def _run_group(x3, mods, kvmod, s_re0, s_im0, attend, p, nb):
    nbatch, t, d = x3.shape
    m = nbatch * t
    x = x3.reshape(m, d)
    depth = p['norm_w'].shape[0]
    n_a = p['lam_re'].shape[0]
    g, n = p['lam_re'].shape[1:]
    tv = min(SSM_CHUNK, t)
    assert t % tv == 0
    nc = t // tv
    new_re, new_im = [], []
    k = v = kb = vb = None
    for l in range(depth):
        sh1, sc1, g1, sh2, sc2, g2 = mods[l]
        nw1 = p['norm_w'][l, 0][None]
        if l < n_a:
            tabs = _ssm_tables(p['lam_re'][l], p['lam_im'][l], p['log_step'][l], p['b_re'][l], p['b_im'][l],
                               p['c_re'][l], p['c_im'][l], tv, nc)
            h = _norm_mod_call(x, nw1, sh1, sc1, nb)
            s0 = jnp.concatenate([jnp.transpose(s_re0[l], (1, 0, 2)), jnp.transpose(s_im0[l], (1, 0, 2))], -1)
            y, sf = _ssm_call(h, *tabs, s0.astype(F32), nbatch)
            new_re.append(jnp.transpose(sf[..., :n], (1, 0, 2)))
            new_im.append(jnp.transpose(sf[..., n:], (1, 0, 2)))
            x = _glu_call(x, y, nw1, sh1, sc1, p['ssm_d'][l][None], p['glu_w'][l], g1, nb)
        else:
            j = l - n_a
            lam_init = 0.8 - 0.6 * math.exp(-0.3 * l)
            hd = p['q_norm_w'].shape[1]
            q = _qproj_call(x, nw1, sh1, sc1, p['attn_wq'][j], p['q_norm_w'][j][None], nb,
                            hd ** -0.5 * math.log2(math.e))
            o = attend(q, kb, vb, p['lambda_q1'][j][None], p['lambda_k1'][j][None], p['lambda_q2'][j][None],
                       p['lambda_k2'][j][None], p['subln_w'][j][None], lam_init)
            x = _oproj_call(o, p['attn_wo'][j], x, g1, nb)
        x = _mlp_call(x, p['norm_w'][l, 1][None], sh2, sc2, g2, p['mlp_up'][l], p['mlp_down'][l], nb)
        if l == n_a - 1:
            k, v, kb, vb = _kvproj_call(x, p['kv_norm_w'][None], kvmod[0], kvmod[1], p['kv_w'], p['k_norm_w'][None],
                                        nb)
    return x.reshape(nbatch, t, d), jnp.stack(new_re), jnp.stack(new_im), k, v


def kernel(x_prompt, x_sample, state_ssm_re, state_ssm_im, cache_k, cache_v, page_table, c_prompt, c_sample, ada_w, ada_b, norm_w, mlp_up, mlp_down, ssm_lambda_re, ssm_lambda_im, ssm_log_step, ssm_b_re, ssm_b_im, ssm_c_re, ssm_c_im, ssm_d, glu_w, kv_ada_w, kv_ada_b, kv_norm_w, kv_w, k_norm_w, attn_wq, q_norm_w, lambda_q1, lambda_k1, lambda_q2, lambda_k2, subln_w, attn_wo):
    bp, tp, d = x_prompt.shape
    bs, ts, _ = x_sample.shape
    depth = ada_w.shape[0]
    n_a = ssm_lambda_re.shape[0]
    nh, vd = cache_v.shape[2], cache_v.shape[3]
    hd = q_norm_w.shape[1]

    p = {
        'norm_w': norm_w, 'mlp_up': mlp_up.astype(BF16), 'mlp_down': mlp_down.astype(BF16),
        'lam_re': ssm_lambda_re, 'lam_im': ssm_lambda_im, 'log_step': ssm_log_step,
        'b_re': ssm_b_re.astype(F32), 'b_im': ssm_b_im.astype(F32),
        'c_re': ssm_c_re.astype(F32), 'c_im': ssm_c_im.astype(F32),
        'ssm_d': ssm_d, 'glu_w': glu_w.astype(BF16), 'kv_norm_w': kv_norm_w, 'kv_w': kv_w.astype(BF16),
        'k_norm_w': k_norm_w, 'attn_wq': attn_wq.astype(BF16), 'q_norm_w': q_norm_w,
        'lambda_q1': lambda_q1, 'lambda_k1': lambda_k1, 'lambda_q2': lambda_q2, 'lambda_k2': lambda_k2,
        'subln_w': subln_w, 'attn_wo': attn_wo.astype(BF16),
    }

    nc_rows = bp + bs
    c_all = jnp.concatenate([c_prompt, c_sample], axis=0)
    c_all = jnp.pad(c_all, ((0, -nc_rows % 16), (0, 0))).astype(BF16)
    mod_all = _ada(c_all, ada_w, ada_b[:, None, :])
    kvmod_all = _ada(c_all, kv_ada_w[None], kv_ada_b[None, None, :])[0]

    def split_mods(lo, hi, rep):
        def shape(a):
            return a[:, None, :] if rep == 1 else jnp.repeat(a, rep, axis=0)[None]
        mods = [[shape(mod_all[l, lo:hi, i * d:(i + 1) * d]) for i in range(6)] for l in range(depth)]
        kvmod = [shape(kvmod_all[lo:hi, i * d:(i + 1) * d]) for i in range(2)]
        return mods, kvmod

    mods_p, kvmod_p = split_mods(0, bp, 1)
    zeros = jnp.zeros((n_a, bp) + ssm_lambda_re.shape[1:], F32)

    def attend_prompt(q, kb, vb, lq1, lk1, lq2, lk2, subw, lam_init):
        return _pattn_call(q, kb, vb, lq1, lk1, lq2, lk2, subw, bp, lam_init)

    y_p, re_p, im_p, k_p, v_p = _run_group(x_prompt, mods_p, kvmod_p, zeros, zeros, attend_prompt, p, bp)

    mods_s, kvmod_s = split_mods(bp, bp + bs, ts)
    page = cache_k.shape[1]
    ck = cache_k.reshape(cache_k.shape[0], page * nh, 2 * hd)
    cv = cache_v.reshape(cache_v.shape[0], page * nh, vd)
    eye2 = jnp.eye(2, dtype=BF16)
    ncol = nh * 2 * ts
    jp = -(-ncol // 128) * 128

    def attend_sample(q, kb, vb, lq1, lk1, lq2, lk2, subw, lam_init):
        q5 = q.reshape(bs, ts, nh, 2, hd)
        qw = jnp.einsum('bihmd,mM->bmdhMi', q5, eye2).reshape(bs, 2 * hd, ncol)
        qw = jnp.pad(qw, ((0, 0), (0, 0), (0, jp - ncol)))
        k_new = kb.reshape(bs, ts * nh, 2 * hd)
        v_new = vb.reshape(bs, ts * nh, vd)
        ot = _sattn_call(page_table, qw, ck, cv, k_new, v_new, lq1, lk1, lq2, lk2, subw.reshape(vd, 1),
                         nh, ts, ncol, lam_init)
        o = ot[:, :, :ncol].reshape(bs, vd, nh, 2, ts)[:, :, :, 0, :]
        return jnp.transpose(o, (0, 3, 2, 1)).reshape(bs * ts, d).astype(BF16)

    y_s, re_s, im_s, k_s, v_s = _run_group(x_sample, mods_s, kvmod_s, state_ssm_re, state_ssm_im, attend_sample,
                                           p, 1)

    return (y_p, y_s, re_p, im_p,
            k_p.reshape(bp, tp, nh, 2 * hd), v_p.reshape(bp, tp, nh, vd),
            re_s, im_s,
            k_s.reshape(bs, ts, nh, 2 * hd), v_s.reshape(bs, ts, nh, vd))
```

```python
import functools
import math

import jax
import jax.numpy as jnp
from jax import lax
from jax.experimental import pallas as pl
from jax.experimental.pallas import tpu as pltpu

F32 = jnp.float32
BF16 = jnp.bfloat16
HI = lax.Precision.HIGHEST

NORM_EPS = 1e-6
SSM_CHUNK = 16
SLAB = 256
S5_BLOCK_CHUNKS = 256
MIB = 1024 * 1024
VMEM_LIMIT_BYTES = 52 * MIB
ATTN_TILE = 512


def _params(*sem):
    return pltpu.CompilerParams(dimension_semantics=sem, vmem_limit_bytes=VMEM_LIMIT_BYTES)


def _tile(dim, pref):
    t = min(dim, pref)
    assert dim % t == 0, (dim, pref)
    return t


def _dot(a, b):
    return jnp.dot(a, b, preferred_element_type=F32)


def _dot_nt(a, b):
    return lax.dot_general(a, b, (((1,), (1,)), ((), ())), preferred_element_type=F32)


def _rms(x, w):
    return x * lax.rsqrt(jnp.mean(x * x, axis=-1, keepdims=True) + NORM_EPS) * w


def _norm_mod(x, nw, sh, sc):
    return _rms(x, nw) * (1.0 + sc) + sh


def _chunk_rms(y, w, width):
    parts = []
    for c in range(y.shape[-1] // width):
        parts.append(_rms(y[:, c * width:(c + 1) * width], w))
    return parts[0] if len(parts) == 1 else jnp.concatenate(parts, axis=-1)


def _ada_kernel(c_ref, w_ref, b_ref, o_ref):
    o_ref[0] = _dot(c_ref[...], w_ref[0].astype(BF16)) + b_ref[0]


def _ada(c, w, b):
    nl, d, n = w.shape
    r = c.shape[0]
    tn = _tile(n, 512)
    return pl.pallas_call(
        _ada_kernel,
        grid=(nl, n // tn),
        in_specs=[pl.BlockSpec((r, d), lambda l, j: (0, 0)),
                  pl.BlockSpec((1, d, tn), lambda l, j: (l, 0, j)),
                  pl.BlockSpec((1, 1, tn), lambda l, j: (l, 0, j))],
        out_specs=pl.BlockSpec((1, r, tn), lambda l, j: (l, 0, j)),
        out_shape=jax.ShapeDtypeStruct((nl, r, n), F32),
        compiler_params=_params("parallel", "parallel"),
        name="ada",
    )(c, w, b)


class _Rows:
    def __init__(self, m, nb, tm_pref):
        self.m = m
        self.nb = nb
        rows_per_nb = m // nb
        self.tm = _tile(rows_per_nb, tm_pref)
        self.tiles_per_nb = rows_per_nb // self.tm
        self.nt = m // self.tm

    def mod_spec(self, r, width, col=None):
        tpn = self.tiles_per_nb
        if col is None:
            return pl.BlockSpec((1, r, width), lambda i, j: (i // tpn, 0, 0))
        return pl.BlockSpec((1, r, width), lambda i, j: (i // tpn, 0, j))


def _norm_mod_kernel(x_ref, nw_ref, sh_ref, sc_ref, o_ref):
    o_ref[...] = _norm_mod(x_ref[...], nw_ref[...], sh_ref[0], sc_ref[0]).astype(BF16)


def _norm_mod_call(x, nw, sh, sc, nb):
    m, d = x.shape
    rows = _Rows(m, nb, 512)
    tm, r = rows.tm, sh.shape[1]
    tpn = rows.tiles_per_nb
    return pl.pallas_call(
        _norm_mod_kernel,
        grid=(rows.nt,),
        in_specs=[pl.BlockSpec((tm, d), lambda i: (i, 0)),
                  pl.BlockSpec((1, d), lambda i: (0, 0)),
                  pl.BlockSpec((1, r, d), lambda i: (i // tpn, 0, 0)),
                  pl.BlockSpec((1, r, d), lambda i: (i // tpn, 0, 0))],
        out_specs=pl.BlockSpec((tm, d), lambda i: (i, 0)),
        out_shape=jax.ShapeDtypeStruct((m, d), BF16),
        compiler_params=_params("parallel"),
        name="norm_mod",
    )(x, nw, sh, sc)


def _cmul(a_ref_val, s, half):
    return a_ref_val[0:1, :] * s + a_ref_val[1:2, :] * pltpu.roll(s, half, axis=1)


def _class_perm(tr, cr, ncls, transpose):
    a = lax.broadcasted_iota(jnp.int32, (tr, tr), 1 if transpose else 0)
    b = lax.broadcasted_iota(jnp.int32, (tr, tr), 0 if transpose else 1)
    return (b == _irem(a, cr) * ncls + _idiv(a, cr)).astype(BF16)


def _ssm_kernel(h_ref, kbd_ref, wst_ref, wout_ref, apow_ref, s0_ref, y_ref, sf_ref, u_scr, yc_scr, sin_scr,
                *, ncls, nb, nc, half, unit_lanes, cr):
    r = nb * nc
    n2 = 2 * half
    nsteps = nc.bit_length() - 1
    nunit = SLAB // unit_lanes
    gpu = SLAB // n2
    tr = ncls * cr
    ntile = r // cr
    row = lax.broadcasted_iota(jnp.int32, (r, n2), 0)
    chunk = row & (nc - 1)
    wrow = _idiv(lax.broadcasted_iota(jnp.int32, (SLAB, SLAB), 0), unit_lanes)

    perm = _class_perm(tr, cr, ncls, False)
    for i in range(ntile):
        uc = _dot(perm, h_ref[i * tr:(i + 1) * tr, :]).astype(BF16)
        for k in range(ncls):
            u_scr[k, i * cr:(i + 1) * cr, :] = uc[k * cr:(k + 1) * cr, :]

    def carry_states(p, apow, s0):
        s0_rows = jnp.zeros((r, n2), F32)
        for b in range(nb):
            s0_rows = jnp.where(row == b * nc, s0[b:b + 1, :], s0_rows)
        p = p + _cmul(apow[0], s0_rows, half)
        for k in range(nsteps):
            d = 1 << k
            shifted = jnp.where(chunk >= d, pltpu.roll(p, d, axis=0), 0.0)
            p = p + _cmul(apow[k], shifted, half)
        s_in = s0_rows if nc == 1 else jnp.where(chunk == 0, s0_rows, pltpu.roll(p, 1, axis=0))
        return s_in, p

    def unit_body(ui, carry):
        sl = None
        for k in range(ncls):
            t = _dot(u_scr[k], jnp.where(wrow == ui, wst_ref[0, k], jnp.zeros((), BF16)))
            sl = t if sl is None else sl + t
        parts = []
        for hf in range(gpu):
            g = ui * gpu + hf
            s_in, p = carry_states(sl[:, hf * n2:(hf + 1) * n2], apow_ref[g], s0_ref[g, 0])
            parts.append(s_in.astype(BF16))
            for b in range(nb):
                last = b * nc + nc - 1
                sf_ref[g, 0, b:b + 1, :] = p[last:last + 1, :]
        sin_scr[ui] = parts[0] if gpu == 1 else jnp.concatenate(parts, axis=1)
        return carry

    lax.fori_loop(0, nunit, unit_body, 0)

    for t in range(ncls):
        acc = None
        for k in range(t + 1):
            d = _dot(u_scr[k], kbd_ref[0, t - k])
            acc = d if acc is None else acc + d
        for ui in range(nunit):
            acc = acc + _dot_nt(sin_scr[ui], jnp.where(wrow == ui, wout_ref[0, t], jnp.zeros((), BF16)))
        yc_scr[t] = acc

    perm_t = _class_perm(tr, cr, ncls, True)
    for i in range(ntile):
        yc = jnp.concatenate([yc_scr[k, i * cr:(i + 1) * cr, :] for k in range(ncls)], axis=0)
        hi = yc.astype(BF16)
        lo = (yc - hi.astype(F32)).astype(BF16)
        y_ref[i * tr:(i + 1) * tr, :] = _dot(perm_t, hi) + _dot(perm_t, lo)


def _ssm_call(h, kbd, wst, wout, apow, s0, nbatch):
    m, d = h.shape
    ncls = kbd.shape[1]
    t = m // nbatch
    nc = t // ncls
    assert nc & (nc - 1) == 0
    g, ns, _, n2 = apow.shape
    nslab = d // SLAB
    gps = g // nslab
    unit_lanes = (SLAB // n2) * (SLAB // gps)
    bpb = max(b for b in range(1, nbatch + 1) if nbatch % b == 0 and b * nc <= S5_BLOCK_CHUNKS)
    nblk = nbatch // bpb
    r = bpb * nc
    cr = min(r, 32)
    kern = functools.partial(_ssm_kernel, ncls=ncls, nb=bpb, nc=nc, half=n2 // 2, unit_lanes=unit_lanes, cr=cr)
    tab = pl.BlockSpec((1, ncls, SLAB, SLAB), lambda i, j: (i, 0, 0, 0))
    st = pl.BlockSpec((gps, 1, bpb, n2), lambda i, j: (i, j, 0, 0))
    y, sf = pl.pallas_call(
        kern,
        grid=(nslab, nblk),
        in_specs=[pl.BlockSpec((bpb * t, SLAB), lambda i, j: (j, i)),
                  tab, tab, tab,
                  pl.BlockSpec((gps, ns, 2, n2), lambda i, j: (i, 0, 0, 0)),
                  st],
        out_specs=[pl.BlockSpec((bpb * t, SLAB), lambda i, j: (j, i)), st],
        out_shape=[jax.ShapeDtypeStruct((m, d), F32),
                   jax.ShapeDtypeStruct((g, nblk, bpb, n2), F32)],
        scratch_shapes=[pltpu.VMEM((ncls, r, SLAB), BF16), pltpu.VMEM((ncls, r, SLAB), F32),
                        pltpu.VMEM((SLAB // unit_lanes, r, SLAB), BF16)],
        compiler_params=_params("parallel", "parallel"),
        name="ssm",
    )(h, kbd, wst, wout, apow, s0.reshape(g, nblk, bpb, n2))
    return y, sf.reshape(g, nbatch, n2)


def _ssm_tables(lam_re, lam_im, log_step, b_re, b_im, c_re, c_im, tv, nc):
    L = tv
    g, n = lam_re.shape
    gc = b_re.shape[2]
    lr, li = lam_re.astype(F32), lam_im.astype(F32)
    dt = jnp.exp(log_step.astype(F32))[:, None]
    mag = jnp.exp(lr * dt)
    ar, ai = mag * jnp.cos(li * dt), mag * jnp.sin(li * dt)
    den = lr * lr + li * li
    xr = ar - 1.0
    f_re = (xr * lr + ai * li) / den
    f_im = (ai * lr - xr * li) / den
    b_ret, b_imt = jnp.swapaxes(b_re, 1, 2), jnp.swapaxes(b_im, 1, 2)
    bb_re = f_re[:, None, :] * b_ret - f_im[:, None, :] * b_imt
    bb_im = f_re[:, None, :] * b_imt + f_im[:, None, :] * b_ret
    j = jnp.arange(L + 1, dtype=F32)[:, None, None]
    pmag = jnp.exp(j * (lr * dt))
    pw_re, pw_im = pmag * jnp.cos(j * (li * dt)), pmag * jnp.sin(j * (li * dt))
    pb_re = pw_re[:, :, None, :] * bb_re - pw_im[:, :, None, :] * bb_im
    pb_im = pw_re[:, :, None, :] * bb_im + pw_im[:, :, None, :] * bb_re
    nslab = g * gc // SLAB
    gps = g // nslab
    n2 = 2 * n
    gpu = SLAB // n2
    rowg = jnp.arange(SLAB)[:, None] // gc
    same_group = rowg == jnp.arange(SLAB)[None, :] // gc
    own_half = rowg % gpu == jnp.arange(SLAB)[None, :] // n2
    kj = (jnp.einsum('gcn,jgdn->jdgc', c_re, pb_re[:L], precision=HI)
          - jnp.einsum('gcn,jgdn->jdgc', c_im, pb_im[:L], precision=HI))
    kj = jnp.transpose(kj.reshape(L, gc, nslab, SLAB), (2, 0, 1, 3))
    kbd = jnp.where(same_group, jnp.tile(kj, (1, 1, gps, 1)), 0.0)

    def per_group_rows(w):
        w = jnp.transpose(w.reshape(L, nslab, SLAB, n2), (1, 0, 2, 3))
        return jnp.where(own_half, jnp.tile(w, (1, 1, 1, gpu)), 0.0)

    wst = per_group_rows(jnp.concatenate([pb_re[:L][::-1], pb_im[:L][::-1]], axis=3))
    cp_re = c_re[None] * pw_re[1:, :, None, :] - c_im[None] * pw_im[1:, :, None, :]
    cp_im = c_re[None] * pw_im[1:, :, None, :] + c_im[None] * pw_re[1:, :, None, :]
    wout = per_group_rows(jnp.concatenate([cp_re, -cp_im], axis=3))
    are, aim = pw_re[tv], pw_im[tv]
    steps = []
    for _ in range(max(nc.bit_length() - 1, 1)):
        steps.append(jnp.stack([jnp.concatenate([are, are], -1), jnp.concatenate([-aim, aim], -1)], axis=1))
        are, aim = are * are - aim * aim, 2.0 * are * aim
    apow = jnp.stack(steps, axis=1)
    return kbd.astype(BF16), wst.astype(BF16), wout.astype(BF16), apow


def _glu_kernel(x_ref, y_ref, nw_ref, sh_ref, sc_ref, d_ref, wv_ref, wg_ref, xr_ref, g_ref, o_ref, a_scr):
    @pl.when(pl.program_id(1) == 0)
    def _():
        h = _norm_mod(x_ref[...], nw_ref[...], sh_ref[0], sc_ref[0])
        a_scr[...] = jax.nn.gelu(y_ref[...] + d_ref[...] * h).astype(BF16)

    a = a_scr[...]
    val = _dot(a, wv_ref[...])
    gate = _dot(a, wg_ref[...])
    o_ref[...] = xr_ref[...] + g_ref[0] * (val / (1.0 + jnp.exp(-gate)))


def _glu_call(x, y, nw, sh, sc, dskip, w, g1, nb):
    m, d = x.shape
    rows = _Rows(m, nb, 512)
    tm, r = rows.tm, sh.shape[1]
    tn = _tile(d, 512)
    nn = d // tn
    return pl.pallas_call(
        _glu_kernel,
        grid=(rows.nt, nn),
        in_specs=[pl.BlockSpec((tm, d), lambda i, j: (i, 0)),
                  pl.BlockSpec((tm, d), lambda i, j: (i, 0)),
                  pl.BlockSpec((1, d), lambda i, j: (0, 0)),
                  rows.mod_spec(r, d), rows.mod_spec(r, d),
                  pl.BlockSpec((1, d), lambda i, j: (0, 0)),
                  pl.BlockSpec((d, tn), lambda i, j: (0, j)),
                  pl.BlockSpec((d, tn), lambda i, j: (0, j + nn)),
                  pl.BlockSpec((tm, tn), lambda i, j: (i, j)),
                  rows.mod_spec(r, tn, col=True)],
        out_specs=pl.BlockSpec((tm, tn), lambda i, j: (i, j)),
        out_shape=jax.ShapeDtypeStruct((m, d), F32),
        scratch_shapes=[pltpu.VMEM((tm, d), BF16)],
        compiler_params=_params("parallel", "arbitrary"),
        name="glu",
    )(x, y, nw, sh, sc, dskip, w, w, x, g1)


def _mlp_kernel(x_ref, nw_ref, sh_ref, sc_ref, g_ref, up_ref, dn_ref, o_ref, h_scr, *, nf):
    f = pl.program_id(1)

    @pl.when(f == 0)
    def _():
        h_scr[...] = _norm_mod(x_ref[...], nw_ref[...], sh_ref[0], sc_ref[0]).astype(BF16)

    a = jnp.maximum(_dot(h_scr[...], up_ref[...]), 0.0)
    part = _dot((a * a).astype(BF16), dn_ref[...])

    @pl.when(f == 0)
    def _():
        o_ref[...] = part

    @pl.when(f > 0)
    def _():
        o_ref[...] += part

    @pl.when(f == nf - 1)
    def _():
        o_ref[...] = x_ref[...] + g_ref[0] * o_ref[...]


def _mlp_call(x, nw, sh, sc, g2, up, dn, nb):
    m, d = x.shape
    ff = up.shape[1]
    rows = _Rows(m, nb, 512)
    tm, r = rows.tm, sh.shape[1]
    tf = _tile(ff, 1024)
    nf = ff // tf
    return pl.pallas_call(
        functools.partial(_mlp_kernel, nf=nf),
        grid=(rows.nt, nf),
        in_specs=[pl.BlockSpec((tm, d), lambda i, j: (i, 0)),
                  pl.BlockSpec((1, d), lambda i, j: (0, 0)),
                  rows.mod_spec(r, d), rows.mod_spec(r, d), rows.mod_spec(r, d),
                  pl.BlockSpec((d, tf), lambda i, j: (0, j)),
                  pl.BlockSpec((tf, d), lambda i, j: (j, 0))],
        out_specs=pl.BlockSpec((tm, d), lambda i, j: (i, 0)),
        out_shape=jax.ShapeDtypeStruct((m, d), F32),
        scratch_shapes=[pltpu.VMEM((tm, d), BF16)],
        compiler_params=_params("parallel", "arbitrary"),
        name="mlp",
    )(x, nw, sh, sc, g2, up, dn)


def _mod_rows(ref, r0, ts):
    return ref[0] if ref.shape[1] == 1 else ref[0, r0:r0 + ts, :]


def _qproj_kernel(x_ref, nw_ref, sh_ref, sc_ref, w_ref, cw_ref, q_ref, *, hd, ts, qscale):
    cw = cw_ref[...] * qscale
    for r0 in range(0, x_ref.shape[0], ts):
        h = _norm_mod(x_ref[r0:r0 + ts, :], nw_ref[...], _mod_rows(sh_ref, r0, ts), _mod_rows(sc_ref, r0, ts))
        q = _dot(h.astype(BF16), w_ref[...])
        for c in range(q.shape[1] // hd):
            q_ref[r0:r0 + ts, c * hd:(c + 1) * hd] = _rms(q[:, c * hd:(c + 1) * hd], cw).astype(BF16)


def _qproj_call(x, nw, sh, sc, wq, qnw, nb, qscale):
    m, d = x.shape
    n = wq.shape[1]
    hd = qnw.shape[1]
    rows = _Rows(m, nb, 512)
    tm, r = rows.tm, sh.shape[1]
    tpn = rows.tiles_per_nb
    mod = pl.BlockSpec((1, r, d), lambda i: (i // tpn, 0, 0))
    return pl.pallas_call(
        functools.partial(_qproj_kernel, hd=hd, ts=min(tm, 256), qscale=qscale),
        grid=(rows.nt,),
        in_specs=[pl.BlockSpec((tm, d), lambda i: (i, 0)),
                  pl.BlockSpec((1, d), lambda i: (0, 0)),
                  mod, mod,
                  pl.BlockSpec((d, n), lambda i: (0, 0)),
                  pl.BlockSpec((1, hd), lambda i: (0, 0))],
        out_specs=pl.BlockSpec((tm, n), lambda i: (i, 0)),
        out_shape=jax.ShapeDtypeStruct((m, n), BF16),
        compiler_params=_params("parallel"),
        name="q_proj",
    )(x, nw, sh, sc, wq, qnw)


def _kvproj_kernel(x_ref, nw_ref, sh_ref, sc_ref, wk_ref, wv_ref, cw_ref, k_ref, v_ref, kb_ref, vb_ref, h_scr,
                   *, hd):
    @pl.when(pl.program_id(1) == 0)
    def _():
        h_scr[...] = _norm_mod(x_ref[...], nw_ref[...], sh_ref[0], sc_ref[0]).astype(BF16)

    h = h_scr[...]
    k = _chunk_rms(_dot(h, wk_ref[...]), cw_ref[...], hd)
    v = _dot(h, wv_ref[...])
    k_ref[...] = k
    v_ref[...] = v
    kb_ref[...] = k.astype(BF16)
    vb_ref[...] = v.astype(BF16)


def _kvproj_call(x, nw, sh, sc, w, knw, nb):
    m, d = x.shape
    nk = w.shape[1] // 2
    hd = knw.shape[1]
    rows = _Rows(m, nb, 512)
    tm, r = rows.tm, sh.shape[1]
    tn = _tile(nk, 512)
    nn = nk // tn
    out_spec = pl.BlockSpec((tm, tn), lambda i, j: (i, j))
    return pl.pallas_call(
        functools.partial(_kvproj_kernel, hd=hd),
        grid=(rows.nt, nn),
        in_specs=[pl.BlockSpec((tm, d), lambda i, j: (i, 0)),
                  pl.BlockSpec((1, d), lambda i, j: (0, 0)),
                  rows.mod_spec(r, d), rows.mod_spec(r, d),
                  pl.BlockSpec((d, tn), lambda i, j: (0, j)),
                  pl.BlockSpec((d, tn), lambda i, j: (0, j + nn)),
                  pl.BlockSpec((1, hd), lambda i, j: (0, 0))],
        out_specs=[out_spec, out_spec, out_spec, out_spec],
        out_shape=[jax.ShapeDtypeStruct((m, nk), F32), jax.ShapeDtypeStruct((m, nk), F32),
                   jax.ShapeDtypeStruct((m, nk), BF16), jax.ShapeDtypeStruct((m, nk), BF16)],
        scratch_shapes=[pltpu.VMEM((tm, d), BF16)],
        compiler_params=_params("parallel", "arbitrary"),
        name="kv_proj",
    )(x, nw, sh, sc, w, w, knw)


def _oproj_kernel(o_ref, w_ref, x_ref, g_ref, y_ref):
    y_ref[...] = x_ref[...] + g_ref[0] * _dot(o_ref[...], w_ref[...])


def _oproj_call(o, wo, x, g1, nb):
    m, d = x.shape
    rows = _Rows(m, nb, 512)
    tm, r = rows.tm, g1.shape[1]
    tpn = rows.tiles_per_nb
    return pl.pallas_call(
        _oproj_kernel,
        grid=(rows.nt,),
        in_specs=[pl.BlockSpec((tm, o.shape[1]), lambda i: (i, 0)),
                  pl.BlockSpec((o.shape[1], d), lambda i: (0, 0)),
                  pl.BlockSpec((tm, d), lambda i: (i, 0)),
                  pl.BlockSpec((1, r, d), lambda i: (i // tpn, 0, 0))],
        out_specs=pl.BlockSpec((tm, d), lambda i: (i, 0)),
        out_shape=jax.ShapeDtypeStruct((m, d), F32),
        compiler_params=_params("parallel"),
        name="o_proj",
    )(o, wo, x, g1)


def _lambda(lq1, lk1, lq2, lk2, lam_init):
    return (jnp.exp(jnp.sum(lq1 * lk1, axis=-1, keepdims=True))
            - jnp.exp(jnp.sum(lq2 * lk2, axis=-1, keepdims=True)) + lam_init)


def _pattn_kernel(q_ref, k_ref, v_ref, lq1_ref, lk1_ref, lq2_ref, lk2_ref, swc_ref, o_ref, m_scr, l_scr, acc_scr,
                  *, hd, tq, lam_init):
    qi = pl.program_id(2)
    qt = q_ref[...].astype(F32).T
    top = lax.broadcasted_iota(jnp.int32, qt.shape, 0) < hd
    qw = jnp.concatenate([jnp.where(top, qt, 0.0), jnp.where(top, 0.0, qt)], axis=1).astype(BF16)
    m_scr[...] = jnp.full(m_scr.shape, -jnp.inf, F32)
    l_scr[...] = jnp.zeros(l_scr.shape, F32)
    acc_scr[...] = jnp.zeros(acc_scr.shape, F32)

    def block(kstart, masked):
        k = k_ref[pl.ds(kstart, tq), :]
        v = v_ref[pl.ds(kstart, tq), :]
        s = _dot(k, qw)
        if masked:
            key = lax.broadcasted_iota(jnp.int32, s.shape, 0)
            qry = _irem(lax.broadcasted_iota(jnp.int32, s.shape, 1), tq)
            s = jnp.where(key <= qry, s, -jnp.inf)
        m_prev = m_scr[...]
        m_new = jnp.maximum(m_prev, jnp.max(s, axis=0, keepdims=True))
        alpha = jnp.exp2(m_prev - m_new)
        p = jnp.exp2(s - m_new)
        l_scr[...] = alpha * l_scr[...] + jnp.sum(p, axis=0, keepdims=True)
        pv = lax.dot_general(v, p.astype(BF16), (((0,), (0,)), ((), ())), preferred_element_type=F32)
        acc_scr[...] = alpha * acc_scr[...] + pv
        m_scr[...] = m_new

    def body(j, carry):
        block(pl.multiple_of(j * tq, tq), False)
        return carry

    lax.fori_loop(0, qi, body, 0)
    block(pl.multiple_of(qi * tq, tq), True)
    lam = _lambda(lq1_ref[...], lk1_ref[...], lq2_ref[...], lk2_ref[...], lam_init)
    on = acc_scr[...] / l_scr[...]
    o = on[:, :tq] - lam * on[:, tq:]
    ms = jnp.mean(o * o, axis=0, keepdims=True)
    o = o * lax.rsqrt(ms + NORM_EPS) * swc_ref[...] * (1.0 - lam_init)
    o_ref[...] = o.T.astype(BF16)


def _pattn_call(q, kb, vb, lq1, lk1, lq2, lk2, subw, nbatch, lam_init):
    m, d = q.shape
    vd = subw.shape[1]
    hd = lq1.shape[1]
    nh = d // vd
    t = m // nbatch
    tq = _tile(t, ATTN_TILE)
    nq = t // tq
    kern = functools.partial(_pattn_kernel, hd=hd, tq=tq, lam_init=lam_init)
    vec = pl.BlockSpec((1, hd), lambda b, h, i: (0, 0))
    return pl.pallas_call(
        kern,
        grid=(nbatch, nh, nq),
        in_specs=[pl.BlockSpec((tq, vd), lambda b, h, i: (b * nq + i, h)),
                  pl.BlockSpec((t, vd), lambda b, h, i: (b, h)),
                  pl.BlockSpec((t, vd), lambda b, h, i: (b, h)),
                  vec, vec, vec, vec,
                  pl.BlockSpec((vd, 1), lambda b, h, i: (0, 0))],
        out_specs=pl.BlockSpec((tq, vd), lambda b, h, i: (b * nq + i, h)),
        out_shape=jax.ShapeDtypeStruct((m, d), BF16),
        scratch_shapes=[pltpu.VMEM((1, 2 * tq), F32), pltpu.VMEM((1, 2 * tq), F32), pltpu.VMEM((vd, 2 * tq), F32)],
        compiler_params=_params("parallel", "parallel", "parallel"),
        name="prompt_attn",
    )(q, kb, vb, lq1, lk1, lq2, lk2, subw.reshape(vd, 1))


def _idiv(x, n):
    if n & (n - 1) == 0:
        return lax.shift_right_logical(x, jnp.full_like(x, n.bit_length() - 1))
    return lax.div(x, jnp.full_like(x, n))


def _irem(x, n):
    if n & (n - 1) == 0:
        return x & (n - 1)
    return lax.rem(x, jnp.full_like(x, n))


def _sattn_kernel(pt_ref, qw_ref, *refs, npg, nh, nq, ncol, lam_init):
    kp_refs = refs[:npg]
    vp_refs = refs[npg:2 * npg]
    kn_ref, vn_ref, lq1_ref, lk1_ref, lq2_ref, lk2_ref, swc_ref, o_ref, m_scr, l_scr, acc_scr = refs[2 * npg:]
    s_id = pl.program_id(1)
    qw = qw_ref[0]
    jp = qw.shape[1]

    @pl.when(s_id == 0)
    def _():
        m_scr[...] = jnp.full(m_scr.shape, -jnp.inf, F32)
        l_scr[...] = jnp.zeros(l_scr.shape, F32)
        acc_scr[...] = jnp.zeros(acc_scr.shape, F32)

    def same_head(nrows):
        row = lax.broadcasted_iota(jnp.int32, (nrows, jp), 0)
        col = lax.broadcasted_iota(jnp.int32, (nrows, jp), 1)
        return row, col, (_irem(row, nh) == _idiv(col, 2 * nq)) | (col >= ncol)

    def update(k, v, keep):
        s = jnp.where(keep, _dot(k, qw), -jnp.inf)
        m_prev = m_scr[...]
        m_new = jnp.maximum(m_prev, jnp.max(s, axis=0, keepdims=True))
        alpha = jnp.exp2(m_prev - m_new)
        p = jnp.exp2(s - m_new)
        l_scr[...] = alpha * l_scr[...] + jnp.sum(p, axis=0, keepdims=True)
        pv = lax.dot_general(v, p.astype(BF16), (((0,), (0,)), ((), ())), preferred_element_type=F32)
        acc_scr[...] = alpha * acc_scr[...] + pv
        m_scr[...] = m_new

    _, _, keep_page = same_head(kp_refs[0].shape[1])
    for j in range(npg):
        update(kp_refs[j][0].astype(BF16), vp_refs[j][0].astype(BF16), keep_page)

    @pl.when(s_id == pl.num_programs(1) - 1)
    def _():
        row, col, keep = same_head(kn_ref.shape[1])
        update(kn_ref[0], vn_ref[0], keep & ((_idiv(row, nh) <= _irem(col, nq)) | (col >= ncol)))
        lam = _lambda(lq1_ref[...], lk1_ref[...], lq2_ref[...], lk2_ref[...], lam_init)
        on = acc_scr[...] / l_scr[...]
        o = on - lam * pltpu.roll(on, jp - nq, axis=1)
        ms = jnp.mean(o * o, axis=0, keepdims=True)
        o_ref[0] = o * lax.rsqrt(ms + NORM_EPS) * swc_ref[...] * (1.0 - lam_init)


def _sattn_call(page_table, qw, cache_k, cache_v, k_new, v_new, lq1, lk1, lq2, lk2, subw_col, nh, nq, ncol,
                lam_init):
    nbatch, kd, jp = qw.shape
    prow = cache_k.shape[1]
    vd = cache_v.shape[2]
    hd = lq1.shape[1]
    npages = page_table.shape[1]
    npg = _tile(npages, 8)
    nk = k_new.shape[1]
    kern = functools.partial(_sattn_kernel, npg=npg, nh=nh, nq=nq, ncol=ncol, lam_init=lam_init)

    def page_spec(j, width):
        return pl.BlockSpec((1, prow, width), lambda b, s, pt: (pt[b, s * npg + j], 0, 0))

    vec = pl.BlockSpec((1, hd), lambda b, s, pt: (0, 0))
    grid_spec = pltpu.PrefetchScalarGridSpec(
        num_scalar_prefetch=1,
        grid=(nbatch, npages // npg),
        in_specs=([pl.BlockSpec((1, kd, jp), lambda b, s, pt: (b, 0, 0))]
                  + [page_spec(j, kd) for j in range(npg)] + [page_spec(j, vd) for j in range(npg)]
                  + [pl.BlockSpec((1, nk, kd), lambda b, s, pt: (b, 0, 0)),
                     pl.BlockSpec((1, nk, vd), lambda b, s, pt: (b, 0, 0)),
                     vec, vec, vec, vec,
                     pl.BlockSpec((vd, 1), lambda b, s, pt: (0, 0))]),
        out_specs=pl.BlockSpec((1, vd, jp), lambda b, s, pt: (b, 0, 0)),
        scratch_shapes=[pltpu.VMEM((1, jp), F32), pltpu.VMEM((1, jp), F32), pltpu.VMEM((vd, jp), F32)],
    )
    return pl.pallas_call(
        kern,
        grid_spec=grid_spec,
        out_shape=jax.ShapeDtypeStruct((nbatch, vd, jp), F32),
        compiler_params=_params("parallel", "arbitrary"),
        name="sample_attn",
    )(page_table, qw, *([cache_k] * npg), *([cache_v] * npg), k_new, v_new, lq1, lk1, lq2, lk2, subw_col)


def _run_group(x3, mods, kvmod, s_re0, s_im0, attend, p, nb):
    nbatch, t, d = x3.shape
    m = nbatch * t
    x = x3.reshape(m, d)
    depth = p['norm_w'].shape[0]
    n_a = p['lam_re'].shape[0]
    g, n = p['lam_re'].shape[1:]
    tv = min(SSM_CHUNK, t)
    assert t % tv == 0
    nc = t // tv
    new_re, new_im = [], []
    k = v = kb = vb = None
    for l in range(depth):
        sh1, sc1, g1, sh2, sc2, g2 = mods[l]
        nw1 = p['norm_w'][l, 0][None]
        if l < n_a:
            tabs = _ssm_tables(p['lam_re'][l], p['lam_im'][l], p['log_step'][l], p['b_re'][l], p['b_im'][l],
                               p['c_re'][l], p['c_im'][l], tv, nc)
            h = _norm_mod_call(x, nw1, sh1, sc1, nb)
            s0 = jnp.concatenate([jnp.transpose(s_re0[l], (1, 0, 2)), jnp.transpose(s_im0[l], (1, 0, 2))], -1)
            y, sf = _ssm_call(h, *tabs, s0.astype(F32), nbatch)
            new_re.append(jnp.transpose(sf[..., :n], (1, 0, 2)))
            new_im.append(jnp.transpose(sf[..., n:], (1, 0, 2)))
            x = _glu_call(x, y, nw1, sh1, sc1, p['ssm_d'][l][None], p['glu_w'][l], g1, nb)
        else:
            j = l - n_a
            lam_init = 0.8 - 0.6 * math.exp(-0.3 * l)
            hd = p['q_norm_w'].shape[1]
            q = _qproj_call(x, nw1, sh1, sc1, p['attn_wq'][j], p['q_norm_w'][j][None], nb,
                            hd ** -0.5 * math.log2(math.e))
            o = attend(q, kb, vb, p['lambda_q1'][j][None], p['lambda_k1'][j][None], p['lambda_q2'][j][None],
                       p['lambda_k2'][j][None], p['subln_w'][j][None], lam_init)
            x = _oproj_call(o, p['attn_wo'][j], x, g1, nb)
        x = _mlp_call(x, p['norm_w'][l, 1][None], sh2, sc2, g2, p['mlp_up'][l], p['mlp_down'][l], nb)
        if l == n_a - 1:
            k, v, kb, vb = _kvproj_call(x, p['kv_norm_w'][None], kvmod[0], kvmod[1], p['kv_w'], p['k_norm_w'][None],
                                        nb)
    return x.reshape(nbatch, t, d), jnp.stack(new_re), jnp.stack(new_im), k, v


def kernel(x_prompt, x_sample, state_ssm_re, state_ssm_im, cache_k, cache_v, page_table, c_prompt, c_sample, ada_w, ada_b, norm_w, mlp_up, mlp_down, ssm_lambda_re, ssm_lambda_im, ssm_log_step, ssm_b_re, ssm_b_im, ssm_c_re, ssm_c_im, ssm_d, glu_w, kv_ada_w, kv_ada_b, kv_norm_w, kv_w, k_norm_w, attn_wq, q_norm_w, lambda_q1, lambda_k1, lambda_q2, lambda_k2, subln_w, attn_wo):
    bp, tp, d = x_prompt.shape
    bs, ts, _ = x_sample.shape
    depth = ada_w.shape[0]
    n_a = ssm_lambda_re.shape[0]
    nh, vd = cache_v.shape[2], cache_v.shape[3]
    hd = q_norm_w.shape[1]

    p = {
        'norm_w': norm_w, 'mlp_up': mlp_up.astype(BF16), 'mlp_down': mlp_down.astype(BF16),
        'lam_re': ssm_lambda_re, 'lam_im': ssm_lambda_im, 'log_step': ssm_log_step,
        'b_re': ssm_b_re.astype(F32), 'b_im': ssm_b_im.astype(F32),
        'c_re': ssm_c_re.astype(F32), 'c_im': ssm_c_im.astype(F32),
        'ssm_d': ssm_d, 'glu_w': glu_w.astype(BF16), 'kv_norm_w': kv_norm_w, 'kv_w': kv_w.astype(BF16),
        'k_norm_w': k_norm_w, 'attn_wq': attn_wq.astype(BF16), 'q_norm_w': q_norm_w,
        'lambda_q1': lambda_q1, 'lambda_k1': lambda_k1, 'lambda_q2': lambda_q2, 'lambda_k2': lambda_k2,
        'subln_w': subln_w, 'attn_wo': attn_wo.astype(BF16),
    }

    nc_rows = bp + bs
    c_all = jnp.concatenate([c_prompt, c_sample], axis=0)
    c_all = jnp.pad(c_all, ((0, -nc_rows % 16), (0, 0))).astype(BF16)
    mod_all = _ada(c_all, ada_w, ada_b[:, None, :])
    kvmod_all = _ada(c_all, kv_ada_w[None], kv_ada_b[None, None, :])[0]

    def split_mods(lo, hi, rep):
        def shape(a):
            return a[:, None, :] if rep == 1 else jnp.repeat(a, rep, axis=0)[None]
        mods = [[shape(mod_all[l, lo:hi, i * d:(i + 1) * d]) for i in range(6)] for l in range(depth)]
        kvmod = [shape(kvmod_all[lo:hi, i * d:(i + 1) * d]) for i in range(2)]
        return mods, kvmod

    mods_p, kvmod_p = split_mods(0, bp, 1)
    zeros = jnp.zeros((n_a, bp) + ssm_lambda_re.shape[1:], F32)

    def attend_prompt(q, kb, vb, lq1, lk1, lq2, lk2, subw, lam_init):
        return _pattn_call(q, kb, vb, lq1, lk1, lq2, lk2, subw, bp, lam_init)

    y_p, re_p, im_p, k_p, v_p = _run_group(x_prompt, mods_p, kvmod_p, zeros, zeros, attend_prompt, p, bp)

    mods_s, kvmod_s = split_mods(bp, bp + bs, ts)
    page = cache_k.shape[1]
    ck = cache_k.reshape(cache_k.shape[0], page * nh, 2 * hd)
    cv = cache_v.reshape(cache_v.shape[0], page * nh, vd)
    eye2 = jnp.eye(2, dtype=BF16)
    ncol = nh * 2 * ts
    jp = -(-ncol // 128) * 128

    def attend_sample(q, kb, vb, lq1, lk1, lq2, lk2, subw, lam_init):
        q5 = q.reshape(bs, ts, nh, 2, hd)
        qw = jnp.einsum('bihmd,mM->bmdhMi', q5, eye2).reshape(bs, 2 * hd, ncol)
        qw = jnp.pad(qw, ((0, 0), (0, 0), (0, jp - ncol)))
        k_new = kb.reshape(bs, ts * nh, 2 * hd)
        v_new = vb.reshape(bs, ts * nh, vd)
        ot = _sattn_call(page_table, qw, ck, cv, k_new, v_new, lq1, lk1, lq2, lk2, subw.reshape(vd, 1),
                         nh, ts, ncol, lam_init)
        o = ot[:, :, :ncol].reshape(bs, vd, nh, 2, ts)[:, :, :, 0, :]
        return jnp.transpose(o, (0, 3, 2, 1)).reshape(bs * ts, d).astype(BF16)

    y_s, re_s, im_s, k_s, v_s = _run_group(x_sample, mods_s, kvmod_s, state_ssm_re, state_ssm_im, attend_sample,
                                           p, 1)

    return (y_p, y_s, re_p, im_p,
            k_p.reshape(bp, tp, nh, 2 * hd), v_p.reshape(bp, tp, nh, vd),
            re_s, im_s,
            k_s.reshape(bs, ts, nh, 2 * hd), v_s.reshape(bs, ts, nh, vd))
```

```python
import functools
import math

import jax
import jax.numpy as jnp
from jax import lax
from jax.experimental import pallas as pl
from jax.experimental.pallas import tpu as pltpu

F32 = jnp.float32
BF16 = jnp.bfloat16
HI = lax.Precision.HIGHEST

NORM_EPS = 1e-6
SSM_CHUNK = 16
SLAB = 256
S5_BLOCK_CHUNKS = 256
MIB = 1024 * 1024
VMEM_LIMIT_BYTES = 52 * MIB
ATTN_TILE = 512


def _params(*sem):
    return pltpu.CompilerParams(dimension_semantics=sem, vmem_limit_bytes=VMEM_LIMIT_BYTES)


def _tile(dim, pref):
    t = min(dim, pref)
    assert dim % t == 0, (dim, pref)
    return t


def _dot(a, b):
    return jnp.dot(a, b, preferred_element_type=F32)


def _dot_nt(a, b):
    return lax.dot_general(a, b, (((1,), (1,)), ((), ())), preferred_element_type=F32)


def _rms(x, w):
    return x * lax.rsqrt(jnp.mean(x * x, axis=-1, keepdims=True) + NORM_EPS) * w


def _norm_mod(x, nw, sh, sc):
    return _rms(x, nw) * (1.0 + sc) + sh


def _chunk_rms(y, w, width):
    parts = []
    for c in range(y.shape[-1] // width):
        parts.append(_rms(y[:, c * width:(c + 1) * width], w))
    return parts[0] if len(parts) == 1 else jnp.concatenate(parts, axis=-1)


def _ada_kernel(c_ref, w_ref, b_ref, o_ref):
    o_ref[0] = _dot(c_ref[...], w_ref[0].astype(BF16)) + b_ref[0]


def _ada(c, w, b):
    nl, d, n = w.shape
    r = c.shape[0]
    tn = _tile(n, 512)
    return pl.pallas_call(
        _ada_kernel,
        grid=(nl, n // tn),
        in_specs=[pl.BlockSpec((r, d), lambda l, j: (0, 0)),
                  pl.BlockSpec((1, d, tn), lambda l, j: (l, 0, j)),
                  pl.BlockSpec((1, 1, tn), lambda l, j: (l, 0, j))],
        out_specs=pl.BlockSpec((1, r, tn), lambda l, j: (l, 0, j)),
        out_shape=jax.ShapeDtypeStruct((nl, r, n), F32),
        compiler_params=_params("parallel", "parallel"),
        name="ada",
    )(c, w, b)


class _Rows:
    def __init__(self, m, nb, tm_pref):
        self.m = m
        self.nb = nb
        rows_per_nb = m // nb
        self.tm = _tile(rows_per_nb, tm_pref)
        self.tiles_per_nb = rows_per_nb // self.tm
        self.nt = m // self.tm

    def mod_spec(self, r, width, col=None):
        tpn = self.tiles_per_nb
        if col is None:
            return pl.BlockSpec((1, r, width), lambda i, j: (i // tpn, 0, 0))
        return pl.BlockSpec((1, r, width), lambda i, j: (i // tpn, 0, j))


def _norm_mod_kernel(x_ref, nw_ref, sh_ref, sc_ref, o_ref):
    o_ref[...] = _norm_mod(x_ref[...], nw_ref[...], sh_ref[0], sc_ref[0]).astype(BF16)


def _norm_mod_call(x, nw, sh, sc, nb):
    m, d = x.shape
    rows = _Rows(m, nb, 512)
    tm, r = rows.tm, sh.shape[1]
    tpn = rows.tiles_per_nb
    return pl.pallas_call(
        _norm_mod_kernel,
        grid=(rows.nt,),
        in_specs=[pl.BlockSpec((tm, d), lambda i: (i, 0)),
                  pl.BlockSpec((1, d), lambda i: (0, 0)),
                  pl.BlockSpec((1, r, d), lambda i: (i // tpn, 0, 0)),
                  pl.BlockSpec((1, r, d), lambda i: (i // tpn, 0, 0))],
        out_specs=pl.BlockSpec((tm, d), lambda i: (i, 0)),
        out_shape=jax.ShapeDtypeStruct((m, d), BF16),
        compiler_params=_params("parallel"),
        name="norm_mod",
    )(x, nw, sh, sc)


def _cmul(a_ref_val, s, half):
    return a_ref_val[0:1, :] * s + a_ref_val[1:2, :] * pltpu.roll(s, half, axis=1)


def _class_perm(tr, cr, ncls, transpose):
    a = lax.broadcasted_iota(jnp.int32, (tr, tr), 1 if transpose else 0)
    b = lax.broadcasted_iota(jnp.int32, (tr, tr), 0 if transpose else 1)
    return (b == _irem(a, cr) * ncls + _idiv(a, cr)).astype(BF16)


def _ssm_kernel(h_ref, kj_ref, ws_ref, wo_ref, apow_ref, s0_ref, y_ref, sf_ref, u_scr, yc_scr, sin_scr, kb_scr,
                *, ncls, nb, nc, half, unit_lanes, cr):
    r = nb * nc
    n2 = 2 * half
    nsteps = nc.bit_length() - 1
    nunit = SLAB // unit_lanes
    gpu = SLAB // n2
    gc = unit_lanes // gpu
    tr = ncls * cr
    ntile = r // cr
    row = lax.broadcasted_iota(jnp.int32, (r, n2), 0)
    chunk = row & (nc - 1)
    wr = lax.broadcasted_iota(jnp.int32, (SLAB, SLAB), 0)
    wc = lax.broadcasted_iota(jnp.int32, (SLAB, SLAB), 1)
    wrow = _idiv(wr, unit_lanes)
    own_state = _irem(_idiv(wr, gc), gpu) == _idiv(wc, n2)
    same_group = _idiv(wr, gc) == _idiv(wc, gc)
    zero = jnp.zeros((), BF16)

    def unit_op(ref_val, keep):
        w = ref_val if gpu == 1 else jnp.concatenate([ref_val] * gpu, axis=1)
        return jnp.where(keep, w, zero)

    for j in range(ncls):
        kb_scr[j] = jnp.where(same_group, jnp.concatenate([kj_ref[0, j]] * (SLAB // gc), axis=0), zero)

    perm = _class_perm(tr, cr, ncls, False)
    for i in range(ntile):
        uc = _dot(perm, h_ref[i * tr:(i + 1) * tr, :]).astype(BF16)
        for k in range(ncls):
            u_scr[k, i * cr:(i + 1) * cr, :] = uc[k * cr:(k + 1) * cr, :]

    def carry_states(p, apow, s0):
        s0_rows = jnp.zeros((r, n2), F32)
        for b in range(nb):
            s0_rows = jnp.where(row == b * nc, s0[b:b + 1, :], s0_rows)
        p = p + _cmul(apow[0], s0_rows, half)
        for k in range(nsteps):
            d = 1 << k
            shifted = jnp.where(chunk >= d, pltpu.roll(p, d, axis=0), 0.0)
            p = p + _cmul(apow[k], shifted, half)
        s_in = s0_rows if nc == 1 else jnp.where(chunk == 0, s0_rows, pltpu.roll(p, 1, axis=0))
        return s_in, p

    def unit_body(ui, carry):
        keep = (wrow == ui) & own_state
        sl = None
        for k in range(ncls):
            t = _dot(u_scr[k], unit_op(ws_ref[0, ncls - 1 - k], keep))
            sl = t if sl is None else sl + t
        parts = []
        for hf in range(gpu):
            g = ui * gpu + hf
            s_in, p = carry_states(sl[:, hf * n2:(hf + 1) * n2], apow_ref[g], s0_ref[g, 0])
            parts.append(s_in.astype(BF16))
            for b in range(nb):
                last = b * nc + nc - 1
                sf_ref[g, 0, b:b + 1, :] = p[last:last + 1, :]
        sin_scr[ui] = parts[0] if gpu == 1 else jnp.concatenate(parts, axis=1)
        return carry

    lax.fori_loop(0, nunit, unit_body, 0)

    keeps = [(wrow == ui) & own_state for ui in range(nunit)]
    for t in range(ncls):
        acc = None
        for k in range(t + 1):
            d = _dot(u_scr[k], kb_scr[t - k])
            acc = d if acc is None else acc + d
        for ui in range(nunit):
            acc = acc + _dot_nt(sin_scr[ui], unit_op(wo_ref[0, t], keeps[ui]))
        yc_scr[t] = acc

    perm_t = _class_perm(tr, cr, ncls, True)
    for i in range(ntile):
        yc = jnp.concatenate([yc_scr[k, i * cr:(i + 1) * cr, :] for k in range(ncls)], axis=0)
        hi = yc.astype(BF16)
        lo = (yc - hi.astype(F32)).astype(BF16)
        y_ref[i * tr:(i + 1) * tr, :] = _dot(perm_t, hi) + _dot(perm_t, lo)


def _ssm_call(h, kj, ws, wo, apow, s0, nbatch, ncls):
    m, d = h.shape
    t = m // nbatch
    nc = t // ncls
    assert nc & (nc - 1) == 0
    g, ns, _, n2 = apow.shape
    nslab = d // SLAB
    gps = g // nslab
    gc = SLAB // gps
    unit_lanes = (SLAB // n2) * gc
    bpb = max(b for b in range(1, nbatch + 1) if nbatch % b == 0 and b * nc <= S5_BLOCK_CHUNKS)
    nblk = nbatch // bpb
    r = bpb * nc
    cr = min(r, 32)
    kern = functools.partial(_ssm_kernel, ncls=ncls, nb=bpb, nc=nc, half=n2 // 2, unit_lanes=unit_lanes, cr=cr)
    st = pl.BlockSpec((gps, 1, bpb, n2), lambda i, j: (i, j, 0, 0))
    y, sf = pl.pallas_call(
        kern,
        grid=(nslab, nblk),
        in_specs=[pl.BlockSpec((bpb * t, SLAB), lambda i, j: (j, i)),
                  pl.BlockSpec((1, ncls, gc, SLAB), lambda i, j: (i, 0, 0, 0)),
                  pl.BlockSpec((1, ncls, SLAB, n2), lambda i, j: (i, 0, 0, 0)),
                  pl.BlockSpec((1, ncls, SLAB, n2), lambda i, j: (i, 0, 0, 0)),
                  pl.BlockSpec((gps, ns, 2, n2), lambda i, j: (i, 0, 0, 0)),
                  st],
        out_specs=[pl.BlockSpec((bpb * t, SLAB), lambda i, j: (j, i)), st],
        out_shape=[jax.ShapeDtypeStruct((m, d), F32),
                   jax.ShapeDtypeStruct((g, nblk, bpb, n2), F32)],
        scratch_shapes=[pltpu.VMEM((ncls, r, SLAB), BF16), pltpu.VMEM((ncls, r, SLAB), F32),
                        pltpu.VMEM((SLAB // unit_lanes, r, SLAB), BF16), pltpu.VMEM((ncls, SLAB, SLAB), BF16)],
        compiler_params=_params("parallel", "parallel"),
        name="ssm",
    )(h, kj, ws, wo, apow, s0.reshape(g, nblk, bpb, n2))
    return y, sf.reshape(g, nbatch, n2)


def _ssm_tables(lam_re, lam_im, log_step, b_re, b_im, c_re, c_im):
    L = SSM_CHUNK
    g, n = lam_re.shape
    gc = b_re.shape[2]
    lr, li = lam_re.astype(F32), lam_im.astype(F32)
    dt = jnp.exp(log_step.astype(F32))[:, None]
    mag = jnp.exp(lr * dt)
    ar, ai = mag * jnp.cos(li * dt), mag * jnp.sin(li * dt)
    den = lr * lr + li * li
    xr = ar - 1.0
    f_re = (xr * lr + ai * li) / den
    f_im = (ai * lr - xr * li) / den
    b_ret, b_imt = jnp.swapaxes(b_re, 1, 2), jnp.swapaxes(b_im, 1, 2)
    bb_re = f_re[:, None, :] * b_ret - f_im[:, None, :] * b_imt
    bb_im = f_re[:, None, :] * b_imt + f_im[:, None, :] * b_ret
    j = jnp.arange(L + 1, dtype=F32)[:, None, None]
    pmag = jnp.exp(j * (lr * dt))
    pw_re, pw_im = pmag * jnp.cos(j * (li * dt)), pmag * jnp.sin(j * (li * dt))
    pb_re = pw_re[:, :, None, :] * bb_re - pw_im[:, :, None, :] * bb_im
    pb_im = pw_re[:, :, None, :] * bb_im + pw_im[:, :, None, :] * bb_re
    nslab = g * gc // SLAB
    n2 = 2 * n
    kj = (jnp.einsum('gcn,jgdn->jdgc', c_re, pb_re[:L], precision=HI)
          - jnp.einsum('gcn,jgdn->jdgc', c_im, pb_im[:L], precision=HI))
    kj = jnp.transpose(kj.reshape(L, gc, nslab, SLAB), (2, 0, 1, 3))

    def per_slab(w):
        return jnp.transpose(w.reshape(L, nslab, SLAB, n2), (1, 0, 2, 3))

    ws = per_slab(jnp.concatenate([pb_re[:L], pb_im[:L]], axis=3))
    cp_re = c_re[None] * pw_re[1:, :, None, :] - c_im[None] * pw_im[1:, :, None, :]
    cp_im = c_re[None] * pw_im[1:, :, None, :] + c_im[None] * pw_re[1:, :, None, :]
    wo = per_slab(jnp.concatenate([cp_re, -cp_im], axis=3))
    return kj.astype(BF16), ws.astype(BF16), wo.astype(BF16), pw_re, pw_im


def _ssm_state_powers(a_re, a_im, nc):
    steps = []
    for _ in range(max(nc.bit_length() - 1, 1)):
        steps.append(jnp.stack([jnp.concatenate([a_re, a_re], -1), jnp.concatenate([-a_im, a_im], -1)], axis=1))
        a_re, a_im = a_re * a_re - a_im * a_im, 2.0 * a_re * a_im
    return jnp.stack(steps, axis=1)


def _glu_kernel(x_ref, y_ref, nw_ref, sh_ref, sc_ref, d_ref, wv_ref, wg_ref, xr_ref, g_ref, o_ref, a_scr):
    @pl.when(pl.program_id(1) == 0)
    def _():
        h = _norm_mod(x_ref[...], nw_ref[...], sh_ref[0], sc_ref[0])
        a_scr[...] = jax.nn.gelu(y_ref[...] + d_ref[...] * h).astype(BF16)

    a = a_scr[...]
    val = _dot(a, wv_ref[...])
    gate = _dot(a, wg_ref[...])
    o_ref[...] = xr_ref[...] + g_ref[0] * (val / (1.0 + jnp.exp(-gate)))


def _glu_call(x, y, nw, sh, sc, dskip, w, g1, nb):
    m, d = x.shape
    rows = _Rows(m, nb, 512)
    tm, r = rows.tm, sh.shape[1]
    tn = _tile(d, 512)
    nn = d // tn
    return pl.pallas_call(
        _glu_kernel,
        grid=(rows.nt, nn),
        in_specs=[pl.BlockSpec((tm, d), lambda i, j: (i, 0)),
                  pl.BlockSpec((tm, d), lambda i, j: (i, 0)),
                  pl.BlockSpec((1, d), lambda i, j: (0, 0)),
                  rows.mod_spec(r, d), rows.mod_spec(r, d),
                  pl.BlockSpec((1, d), lambda i, j: (0, 0)),
                  pl.BlockSpec((d, tn), lambda i, j: (0, j)),
                  pl.BlockSpec((d, tn), lambda i, j: (0, j + nn)),
                  pl.BlockSpec((tm, tn), lambda i, j: (i, j)),
                  rows.mod_spec(r, tn, col=True)],
        out_specs=pl.BlockSpec((tm, tn), lambda i, j: (i, j)),
        out_shape=jax.ShapeDtypeStruct((m, d), F32),
        scratch_shapes=[pltpu.VMEM((tm, d), BF16)],
        compiler_params=_params("parallel", "arbitrary"),
        name="glu",
    )(x, y, nw, sh, sc, dskip, w, w, x, g1)


def _mlp_kernel(x_ref, nw_ref, sh_ref, sc_ref, g_ref, up_ref, dn_ref, o_ref, h_scr, *, nf):
    f = pl.program_id(1)

    @pl.when(f == 0)
    def _():
        h_scr[...] = _norm_mod(x_ref[...], nw_ref[...], sh_ref[0], sc_ref[0]).astype(BF16)

    a = jnp.maximum(_dot(h_scr[...], up_ref[...]), 0.0)
    part = _dot((a * a).astype(BF16), dn_ref[...])

    @pl.when(f == 0)
    def _():
        o_ref[...] = part

    @pl.when(f > 0)
    def _():
        o_ref[...] += part

    @pl.when(f == nf - 1)
    def _():
        o_ref[...] = x_ref[...] + g_ref[0] * o_ref[...]


def _mlp_call(x, nw, sh, sc, g2, up, dn, nb):
    m, d = x.shape
    ff = up.shape[1]
    rows = _Rows(m, nb, 512)
    tm, r = rows.tm, sh.shape[1]
    tf = _tile(ff, 1024)
    nf = ff // tf
    return pl.pallas_call(
        functools.partial(_mlp_kernel, nf=nf),
        grid=(rows.nt, nf),
        in_specs=[pl.BlockSpec((tm, d), lambda i, j: (i, 0)),
                  pl.BlockSpec((1, d), lambda i, j: (0, 0)),
                  rows.mod_spec(r, d), rows.mod_spec(r, d), rows.mod_spec(r, d),
                  pl.BlockSpec((d, tf), lambda i, j: (0, j)),
                  pl.BlockSpec((tf, d), lambda i, j: (j, 0))],
        out_specs=pl.BlockSpec((tm, d), lambda i, j: (i, 0)),
        out_shape=jax.ShapeDtypeStruct((m, d), F32),
        scratch_shapes=[pltpu.VMEM((tm, d), BF16)],
        compiler_params=_params("parallel", "arbitrary"),
        name="mlp",
    )(x, nw, sh, sc, g2, up, dn)


def _mod_rows(ref, r0, ts):
    return ref[0] if ref.shape[1] == 1 else ref[0, r0:r0 + ts, :]


def _qproj_kernel(x_ref, nw_ref, sh_ref, sc_ref, w_ref, cw_ref, q_ref, *, hd, ts, qscale):
    cw = cw_ref[...] * qscale
    for r0 in range(0, x_ref.shape[0], ts):
        h = _norm_mod(x_ref[r0:r0 + ts, :], nw_ref[...], _mod_rows(sh_ref, r0, ts), _mod_rows(sc_ref, r0, ts))
        q = _dot(h.astype(BF16), w_ref[...])
        for c in range(q.shape[1] // hd):
            q_ref[r0:r0 + ts, c * hd:(c + 1) * hd] = _rms(q[:, c * hd:(c + 1) * hd], cw).astype(BF16)


def _qproj_call(x, nw, sh, sc, wq, qnw, nb, qscale):
    m, d = x.shape
    n = wq.shape[1]
    hd = qnw.shape[1]
    rows = _Rows(m, nb, 512)
    tm, r = rows.tm, sh.shape[1]
    tpn = rows.tiles_per_nb
    mod = pl.BlockSpec((1, r, d), lambda i: (i // tpn, 0, 0))
    return pl.pallas_call(
        functools.partial(_qproj_kernel, hd=hd, ts=min(tm, 256), qscale=qscale),
        grid=(rows.nt,),
        in_specs=[pl.BlockSpec((tm, d), lambda i: (i, 0)),
                  pl.BlockSpec((1, d), lambda i: (0, 0)),
                  mod, mod,
                  pl.BlockSpec((d, n), lambda i: (0, 0)),
                  pl.BlockSpec((1, hd), lambda i: (0, 0))],
        out_specs=pl.BlockSpec((tm, n), lambda i: (i, 0)),
        out_shape=jax.ShapeDtypeStruct((m, n), BF16),
        compiler_params=_params("parallel"),
        name="q_proj",
    )(x, nw, sh, sc, wq, qnw)


def _kvproj_kernel(x_ref, nw_ref, sh_ref, sc_ref, wk_ref, wv_ref, cw_ref, k_ref, v_ref, kb_ref, vb_ref, h_scr,
                   *, hd):
    @pl.when(pl.program_id(1) == 0)
    def _():
        h_scr[...] = _norm_mod(x_ref[...], nw_ref[...], sh_ref[0], sc_ref[0]).astype(BF16)

    h = h_scr[...]
    k = _chunk_rms(_dot(h, wk_ref[...]), cw_ref[...], hd)
    v = _dot(h, wv_ref[...])
    k_ref[...] = k
    v_ref[...] = v
    kb_ref[...] = k.astype(BF16)
    vb_ref[...] = v.astype(BF16)


def _kvproj_call(x, nw, sh, sc, w, knw, nb):
    m, d = x.shape
    nk = w.shape[1] // 2
    hd = knw.shape[1]
    rows = _Rows(m, nb, 512)
    tm, r = rows.tm, sh.shape[1]
    tn = _tile(nk, 512)
    nn = nk // tn
    out_spec = pl.BlockSpec((tm, tn), lambda i, j: (i, j))
    return pl.pallas_call(
        functools.partial(_kvproj_kernel, hd=hd),
        grid=(rows.nt, nn),
        in_specs=[pl.BlockSpec((tm, d), lambda i, j: (i, 0)),
                  pl.BlockSpec((1, d), lambda i, j: (0, 0)),
                  rows.mod_spec(r, d), rows.mod_spec(r, d),
                  pl.BlockSpec((d, tn), lambda i, j: (0, j)),
                  pl.BlockSpec((d, tn), lambda i, j: (0, j + nn)),
                  pl.BlockSpec((1, hd), lambda i, j: (0, 0))],
        out_specs=[out_spec, out_spec, out_spec, out_spec],
        out_shape=[jax.ShapeDtypeStruct((m, nk), F32), jax.ShapeDtypeStruct((m, nk), F32),
                   jax.ShapeDtypeStruct((m, nk), BF16), jax.ShapeDtypeStruct((m, nk), BF16)],
        scratch_shapes=[pltpu.VMEM((tm, d), BF16)],
        compiler_params=_params("parallel", "arbitrary"),
        name="kv_proj",
    )(x, nw, sh, sc, w, w, knw)


def _oproj_kernel(o_ref, w_ref, x_ref, g_ref, y_ref):
    y_ref[...] = x_ref[...] + g_ref[0] * _dot(o_ref[...], w_ref[...])


def _oproj_call(o, wo, x, g1, nb):
    m, d = x.shape
    rows = _Rows(m, nb, 512)
    tm, r = rows.tm, g1.shape[1]
    tpn = rows.tiles_per_nb
    return pl.pallas_call(
        _oproj_kernel,
        grid=(rows.nt,),
        in_specs=[pl.BlockSpec((tm, o.shape[1]), lambda i: (i, 0)),
                  pl.BlockSpec((o.shape[1], d), lambda i: (0, 0)),
                  pl.BlockSpec((tm, d), lambda i: (i, 0)),
                  pl.BlockSpec((1, r, d), lambda i: (i // tpn, 0, 0))],
        out_specs=pl.BlockSpec((tm, d), lambda i: (i, 0)),
        out_shape=jax.ShapeDtypeStruct((m, d), F32),
        compiler_params=_params("parallel"),
        name="o_proj",
    )(o, wo, x, g1)


def _lambda(lq1, lk1, lq2, lk2, lam_init):
    return (jnp.exp(jnp.sum(lq1 * lk1, axis=-1, keepdims=True))
            - jnp.exp(jnp.sum(lq2 * lk2, axis=-1, keepdims=True)) + lam_init)


def _pattn_kernel(q_ref, k_ref, v_ref, lq1_ref, lk1_ref, lq2_ref, lk2_ref, swc_ref, o_ref, m_scr, l_scr, acc_scr,
                  *, hd, tq, lam_init):
    qi = pl.program_id(2)
    qt = q_ref[...].astype(F32).T
    top = lax.broadcasted_iota(jnp.int32, qt.shape, 0) < hd
    qw = jnp.concatenate([jnp.where(top, qt, 0.0), jnp.where(top, 0.0, qt)], axis=1).astype(BF16)
    m_scr[...] = jnp.full(m_scr.shape, -jnp.inf, F32)
    l_scr[...] = jnp.zeros(l_scr.shape, F32)
    acc_scr[...] = jnp.zeros(acc_scr.shape, F32)

    def block(kstart, masked):
        k = k_ref[pl.ds(kstart, tq), :]
        v = v_ref[pl.ds(kstart, tq), :]
        s = _dot(k, qw)
        if masked:
            key = lax.broadcasted_iota(jnp.int32, s.shape, 0)
            qry = _irem(lax.broadcasted_iota(jnp.int32, s.shape, 1), tq)
            s = jnp.where(key <= qry, s, -jnp.inf)
        m_prev = m_scr[...]
        m_new = jnp.maximum(m_prev, jnp.max(s, axis=0, keepdims=True))
        alpha = jnp.exp2(m_prev - m_new)
        p = jnp.exp2(s - m_new)
        l_scr[...] = alpha * l_scr[...] + jnp.sum(p, axis=0, keepdims=True)
        pv = lax.dot_general(v, p.astype(BF16), (((0,), (0,)), ((), ())), preferred_element_type=F32)
        acc_scr[...] = alpha * acc_scr[...] + pv
        m_scr[...] = m_new

    def body(j, carry):
        block(pl.multiple_of(j * tq, tq), False)
        return carry

    lax.fori_loop(0, qi, body, 0)
    block(pl.multiple_of(qi * tq, tq), True)
    lam = _lambda(lq1_ref[...], lk1_ref[...], lq2_ref[...], lk2_ref[...], lam_init)
    on = acc_scr[...] / l_scr[...]
    o = on[:, :tq] - lam * on[:, tq:]
    ms = jnp.mean(o * o, axis=0, keepdims=True)
    o = o * lax.rsqrt(ms + NORM_EPS) * swc_ref[...] * (1.0 - lam_init)
    o_ref[...] = o.T.astype(BF16)


def _pattn_call(q, kb, vb, lq1, lk1, lq2, lk2, subw, nbatch, lam_init):
    m, d = q.shape
    vd = subw.shape[1]
    hd = lq1.shape[1]
    nh = d // vd
    t = m // nbatch
    tq = _tile(t, ATTN_TILE)
    nq = t // tq
    kern = functools.partial(_pattn_kernel, hd=hd, tq=tq, lam_init=lam_init)
    vec = pl.BlockSpec((1, hd), lambda b, h, i: (0, 0))
    return pl.pallas_call(
        kern,
        grid=(nbatch, nh, nq),
        in_specs=[pl.BlockSpec((tq, vd), lambda b, h, i: (b * nq + i, h)),
                  pl.BlockSpec((t, vd), lambda b, h, i: (b, h)),
                  pl.BlockSpec((t, vd), lambda b, h, i: (b, h)),
                  vec, vec, vec, vec,
                  pl.BlockSpec((vd, 1), lambda b, h, i: (0, 0))],
        out_specs=pl.BlockSpec((tq, vd), lambda b, h, i: (b * nq + i, h)),
        out_shape=jax.ShapeDtypeStruct((m, d), BF16),
        scratch_shapes=[pltpu.VMEM((1, 2 * tq), F32), pltpu.VMEM((1, 2 * tq), F32), pltpu.VMEM((vd, 2 * tq), F32)],
        compiler_params=_params("parallel", "parallel", "parallel"),
        name="prompt_attn",
    )(q, kb, vb, lq1, lk1, lq2, lk2, subw.reshape(vd, 1))


def _idiv(x, n):
    if n & (n - 1) == 0:
        return lax.shift_right_logical(x, jnp.full_like(x, n.bit_length() - 1))
    return lax.div(x, jnp.full_like(x, n))


def _irem(x, n):
    if n & (n - 1) == 0:
        return x & (n - 1)
    return lax.rem(x, jnp.full_like(x, n))


def _sattn_kernel(pt_ref, qw_ref, *refs, npg, nh, nq, ncol, lam_init):
    kp_refs = refs[:npg]
    vp_refs = refs[npg:2 * npg]
    kn_ref, vn_ref, lq1_ref, lk1_ref, lq2_ref, lk2_ref, swc_ref, o_ref, m_scr, l_scr, acc_scr = refs[2 * npg:]
    s_id = pl.program_id(1)
    qw = qw_ref[0]
    jp = qw.shape[1]

    @pl.when(s_id == 0)
    def _():
        m_scr[...] = jnp.full(m_scr.shape, -jnp.inf, F32)
        l_scr[...] = jnp.zeros(l_scr.shape, F32)
        acc_scr[...] = jnp.zeros(acc_scr.shape, F32)

    def same_head(nrows):
        row = lax.broadcasted_iota(jnp.int32, (nrows, jp), 0)
        col = lax.broadcasted_iota(jnp.int32, (nrows, jp), 1)
        return row, col, (_irem(row, nh) == _idiv(col, 2 * nq)) | (col >= ncol)

    def update(k, v, keep):
        s = jnp.where(keep, _dot(k, qw), -jnp.inf)
        m_prev = m_scr[...]
        m_new = jnp.maximum(m_prev, jnp.max(s, axis=0, keepdims=True))
        alpha = jnp.exp2(m_prev - m_new)
        p = jnp.exp2(s - m_new)
        l_scr[...] = alpha * l_scr[...] + jnp.sum(p, axis=0, keepdims=True)
        pv = lax.dot_general(v, p.astype(BF16), (((0,), (0,)), ((), ())), preferred_element_type=F32)
        acc_scr[...] = alpha * acc_scr[...] + pv
        m_scr[...] = m_new

    _, _, keep_page = same_head(kp_refs[0].shape[1])
    for j in range(npg):
        update(kp_refs[j][0].astype(BF16), vp_refs[j][0].astype(BF16), keep_page)

    @pl.when(s_id == pl.num_programs(1) - 1)
    def _():
        row, col, keep = same_head(kn_ref.shape[1])
        update(kn_ref[0], vn_ref[0], keep & ((_idiv(row, nh) <= _irem(col, nq)) | (col >= ncol)))
        lam = _lambda(lq1_ref[...], lk1_ref[...], lq2_ref[...], lk2_ref[...], lam_init)
        on = acc_scr[...] / l_scr[...]
        o = on - lam * pltpu.roll(on, jp - nq, axis=1)
        ms = jnp.mean(o * o, axis=0, keepdims=True)
        o_ref[0] = o * lax.rsqrt(ms + NORM_EPS) * swc_ref[...] * (1.0 - lam_init)


def _sattn_call(page_table, qw, cache_k, cache_v, k_new, v_new, lq1, lk1, lq2, lk2, subw_col, nh, nq, ncol,
                lam_init):
    nbatch, kd, jp = qw.shape
    prow = cache_k.shape[1]
    vd = cache_v.shape[2]
    hd = lq1.shape[1]
    npages = page_table.shape[1]
    npg = _tile(npages, 8)
    nk = k_new.shape[1]
    kern = functools.partial(_sattn_kernel, npg=npg, nh=nh, nq=nq, ncol=ncol, lam_init=lam_init)

    def page_spec(j, width):
        return pl.BlockSpec((1, prow, width), lambda b, s, pt: (pt[b, s * npg + j], 0, 0))

    vec = pl.BlockSpec((1, hd), lambda b, s, pt: (0, 0))
    grid_spec = pltpu.PrefetchScalarGridSpec(
        num_scalar_prefetch=1,
        grid=(nbatch, npages // npg),
        in_specs=([pl.BlockSpec((1, kd, jp), lambda b, s, pt: (b, 0, 0))]
                  + [page_spec(j, kd) for j in range(npg)] + [page_spec(j, vd) for j in range(npg)]
                  + [pl.BlockSpec((1, nk, kd), lambda b, s, pt: (b, 0, 0)),
                     pl.BlockSpec((1, nk, vd), lambda b, s, pt: (b, 0, 0)),
                     vec, vec, vec, vec,
                     pl.BlockSpec((vd, 1), lambda b, s, pt: (0, 0))]),
        out_specs=pl.BlockSpec((1, vd, jp), lambda b, s, pt: (b, 0, 0)),
        scratch_shapes=[pltpu.VMEM((1, jp), F32), pltpu.VMEM((1, jp), F32), pltpu.VMEM((vd, jp), F32)],
    )
    return pl.pallas_call(
        kern,
        grid_spec=grid_spec,
        out_shape=jax.ShapeDtypeStruct((nbatch, vd, jp), F32),
        compiler_params=_params("parallel", "arbitrary"),
        name="sample_attn",
    )(page_table, qw, *([cache_k] * npg), *([cache_v] * npg), k_new, v_new, lq1, lk1, lq2, lk2, subw_col)


def _run_group(x3, mods, kvmod, s_re0, s_im0, attend, p, nb):
    nbatch, t, d = x3.shape
    m = nbatch * t
    x = x3.reshape(m, d)
    depth = p['norm_w'].shape[0]
    n_a = p['lam_re'].shape[0]
    g, n = p['lam_re'].shape[1:]
    tv = min(SSM_CHUNK, t)
    assert t % tv == 0
    nc = t // tv
    new_re, new_im = [], []
    k = v = kb = vb = None
    for l in range(depth):
        sh1, sc1, g1, sh2, sc2, g2 = mods[l]
        nw1 = p['norm_w'][l, 0][None]
        if l < n_a:
            kj, ws, wo, pw_re, pw_im = p['ssm_tabs'][l]
            apow = _ssm_state_powers(pw_re[tv], pw_im[tv], nc)
            h = _norm_mod_call(x, nw1, sh1, sc1, nb)
            s0 = jnp.concatenate([jnp.transpose(s_re0[l], (1, 0, 2)), jnp.transpose(s_im0[l], (1, 0, 2))], -1)
            y, sf = _ssm_call(h, kj, ws, wo, apow, s0.astype(F32), nbatch, tv)
            new_re.append(jnp.transpose(sf[..., :n], (1, 0, 2)))
            new_im.append(jnp.transpose(sf[..., n:], (1, 0, 2)))
            x = _glu_call(x, y, nw1, sh1, sc1, p['ssm_d'][l][None], p['glu_w'][l], g1, nb)
        else:
            j = l - n_a
            lam_init = 0.8 - 0.6 * math.exp(-0.3 * l)
            hd = p['q_norm_w'].shape[1]
            q = _qproj_call(x, nw1, sh1, sc1, p['attn_wq'][j], p['q_norm_w'][j][None], nb,
                            hd ** -0.5 * math.log2(math.e))
            o = attend(q, kb, vb, p['lambda_q1'][j][None], p['lambda_k1'][j][None], p['lambda_q2'][j][None],
                       p['lambda_k2'][j][None], p['subln_w'][j][None], lam_init)
            x = _oproj_call(o, p['attn_wo'][j], x, g1, nb)
        x = _mlp_call(x, p['norm_w'][l, 1][None], sh2, sc2, g2, p['mlp_up'][l], p['mlp_down'][l], nb)
        if l == n_a - 1:
            k, v, kb, vb = _kvproj_call(x, p['kv_norm_w'][None], kvmod[0], kvmod[1], p['kv_w'], p['k_norm_w'][None],
                                        nb)
    return x.reshape(nbatch, t, d), jnp.stack(new_re), jnp.stack(new_im), k, v


def kernel(x_prompt, x_sample, state_ssm_re, state_ssm_im, cache_k, cache_v, page_table, c_prompt, c_sample, ada_w, ada_b, norm_w, mlp_up, mlp_down, ssm_lambda_re, ssm_lambda_im, ssm_log_step, ssm_b_re, ssm_b_im, ssm_c_re, ssm_c_im, ssm_d, glu_w, kv_ada_w, kv_ada_b, kv_norm_w, kv_w, k_norm_w, attn_wq, q_norm_w, lambda_q1, lambda_k1, lambda_q2, lambda_k2, subln_w, attn_wo):
    bp, tp, d = x_prompt.shape
    bs, ts, _ = x_sample.shape
    depth = ada_w.shape[0]
    n_a = ssm_lambda_re.shape[0]
    nh, vd = cache_v.shape[2], cache_v.shape[3]
    hd = q_norm_w.shape[1]

    p = {
        'norm_w': norm_w, 'mlp_up': mlp_up.astype(BF16), 'mlp_down': mlp_down.astype(BF16),
        'lam_re': ssm_lambda_re,
        'ssm_tabs': [_ssm_tables(ssm_lambda_re[l], ssm_lambda_im[l], ssm_log_step[l], ssm_b_re[l].astype(F32),
                                 ssm_b_im[l].astype(F32), ssm_c_re[l].astype(F32), ssm_c_im[l].astype(F32))
                     for l in range(n_a)],
        'ssm_d': ssm_d, 'glu_w': glu_w.astype(BF16), 'kv_norm_w': kv_norm_w, 'kv_w': kv_w.astype(BF16),
        'k_norm_w': k_norm_w, 'attn_wq': attn_wq.astype(BF16), 'q_norm_w': q_norm_w,
        'lambda_q1': lambda_q1, 'lambda_k1': lambda_k1, 'lambda_q2': lambda_q2, 'lambda_k2': lambda_k2,
        'subln_w': subln_w, 'attn_wo': attn_wo.astype(BF16),
    }

    nc_rows = bp + bs
    c_all = jnp.concatenate([c_prompt, c_sample], axis=0)
    c_all = jnp.pad(c_all, ((0, -nc_rows % 16), (0, 0))).astype(BF16)
    mod_all = _ada(c_all, ada_w, ada_b[:, None, :])
    kvmod_all = _ada(c_all, kv_ada_w[None], kv_ada_b[None, None, :])[0]

    def split_mods(lo, hi, rep):
        def shape(a):
            return a[:, None, :] if rep == 1 else jnp.repeat(a, rep, axis=0)[None]
        mods = [[shape(mod_all[l, lo:hi, i * d:(i + 1) * d]) for i in range(6)] for l in range(depth)]
        kvmod = [shape(kvmod_all[lo:hi, i * d:(i + 1) * d]) for i in range(2)]
        return mods, kvmod

    mods_p, kvmod_p = split_mods(0, bp, 1)
    zeros = jnp.zeros((n_a, bp) + ssm_lambda_re.shape[1:], F32)

    def attend_prompt(q, kb, vb, lq1, lk1, lq2, lk2, subw, lam_init):
        return _pattn_call(q, kb, vb, lq1, lk1, lq2, lk2, subw, bp, lam_init)

    y_p, re_p, im_p, k_p, v_p = _run_group(x_prompt, mods_p, kvmod_p, zeros, zeros, attend_prompt, p, bp)

    mods_s, kvmod_s = split_mods(bp, bp + bs, ts)
    page = cache_k.shape[1]
    ck = cache_k.reshape(cache_k.shape[0], page * nh, 2 * hd)
    cv = cache_v.reshape(cache_v.shape[0], page * nh, vd)
    eye2 = jnp.eye(2, dtype=BF16)
    ncol = nh * 2 * ts
    jp = -(-ncol // 128) * 128

    def attend_sample(q, kb, vb, lq1, lk1, lq2, lk2, subw, lam_init):
        q5 = q.reshape(bs, ts, nh, 2, hd)
        qw = jnp.einsum('bihmd,mM->bmdhMi', q5, eye2).reshape(bs, 2 * hd, ncol)
        qw = jnp.pad(qw, ((0, 0), (0, 0), (0, jp - ncol)))
        k_new = kb.reshape(bs, ts * nh, 2 * hd)
        v_new = vb.reshape(bs, ts * nh, vd)
        ot = _sattn_call(page_table, qw, ck, cv, k_new, v_new, lq1, lk1, lq2, lk2, subw.reshape(vd, 1),
                         nh, ts, ncol, lam_init)
        o = ot[:, :, :ncol].reshape(bs, vd, nh, 2, ts)[:, :, :, 0, :]
        return jnp.transpose(o, (0, 3, 2, 1)).reshape(bs * ts, d).astype(BF16)

    y_s, re_s, im_s, k_s, v_s = _run_group(x_sample, mods_s, kvmod_s, state_ssm_re, state_ssm_im, attend_sample,
                                           p, 1)

    return (y_p, y_s, re_p, im_p,
            k_p.reshape(bp, tp, nh, 2 * hd), v_p.reshape(bp, tp, nh, vd),
            re_s, im_s,
            k_s.reshape(bs, ts, nh, 2 * hd), v_s.reshape(bs, ts, nh, vd))
```

```python
import functools
import math

import jax
import jax.numpy as jnp
from jax import lax
from jax.experimental import pallas as pl
from jax.experimental.pallas import tpu as pltpu

F32 = jnp.float32
BF16 = jnp.bfloat16
HI = lax.Precision.HIGHEST

NORM_EPS = 1e-6
SSM_CHUNK = 16
SLAB = 256
S5_BLOCK_CHUNKS = 256
PROLOGUE_ROWS = 16
MIB = 1024 * 1024
VMEM_LIMIT_BYTES = 52 * MIB
ATTN_TILE = 512


def _params(*sem):
    return pltpu.CompilerParams(dimension_semantics=sem, vmem_limit_bytes=VMEM_LIMIT_BYTES)


def _tile(dim, pref):
    t = min(dim, pref)
    assert dim % t == 0, (dim, pref)
    return t


def _dot(a, b):
    return jnp.dot(a, b, preferred_element_type=F32)


def _dot_nt(a, b):
    return lax.dot_general(a, b, (((1,), (1,)), ((), ())), preferred_element_type=F32)


def _rms(x, w):
    return x * lax.rsqrt(jnp.mean(x * x, axis=-1, keepdims=True) + NORM_EPS) * w


def _norm_mod(x, nw, sh, sc):
    return _rms(x, nw) * (1.0 + sc) + sh


def _chunk_rms(y, w, width):
    parts = []
    for c in range(y.shape[-1] // width):
        parts.append(_rms(y[:, c * width:(c + 1) * width], w))
    return parts[0] if len(parts) == 1 else jnp.concatenate(parts, axis=-1)


def _ada_kernel(c_ref, w_ref, b_ref, o_ref):
    o_ref[0] = _dot(c_ref[...], w_ref[0].astype(BF16)) + b_ref[0]


def _ada(c, w, b):
    nl, d, n = w.shape
    r = c.shape[0]
    tn = _tile(n, 512)
    return pl.pallas_call(
        _ada_kernel,
        grid=(nl, n // tn),
        in_specs=[pl.BlockSpec((r, d), lambda l, j: (0, 0)),
                  pl.BlockSpec((1, d, tn), lambda l, j: (l, 0, j)),
                  pl.BlockSpec((1, 1, tn), lambda l, j: (l, 0, j))],
        out_specs=pl.BlockSpec((1, r, tn), lambda l, j: (l, 0, j)),
        out_shape=jax.ShapeDtypeStruct((nl, r, n), F32),
        compiler_params=_params("parallel", "parallel"),
        name="ada",
    )(c, w, b)


class _Rows:
    def __init__(self, m, nb, tm_pref):
        self.m = m
        self.nb = nb
        rows_per_nb = m // nb
        self.tm = _tile(rows_per_nb, tm_pref)
        self.tiles_per_nb = rows_per_nb // self.tm
        self.nt = m // self.tm

    def mod_spec(self, r, width, col=None):
        tpn = self.tiles_per_nb
        if col is None:
            return pl.BlockSpec((1, r, width), lambda i, j: (i // tpn, 0, 0))
        return pl.BlockSpec((1, r, width), lambda i, j: (i // tpn, 0, j))


def _norm_mod_kernel(x_ref, nw_ref, sh_ref, sc_ref, o_ref):
    o_ref[...] = _norm_mod(x_ref[...], nw_ref[...], sh_ref[0], sc_ref[0]).astype(BF16)


def _norm_mod_call(x, nw, sh, sc, nb):
    m, d = x.shape
    rows = _Rows(m, nb, 512)
    tm, r = rows.tm, sh.shape[1]
    tpn = rows.tiles_per_nb
    return pl.pallas_call(
        _norm_mod_kernel,
        grid=(rows.nt,),
        in_specs=[pl.BlockSpec((tm, d), lambda i: (i, 0)),
                  pl.BlockSpec((1, d), lambda i: (0, 0)),
                  pl.BlockSpec((1, r, d), lambda i: (i // tpn, 0, 0)),
                  pl.BlockSpec((1, r, d), lambda i: (i // tpn, 0, 0))],
        out_specs=pl.BlockSpec((tm, d), lambda i: (i, 0)),
        out_shape=jax.ShapeDtypeStruct((m, d), BF16),
        compiler_params=_params("parallel"),
        name="norm_mod",
    )(x, nw, sh, sc)


def _cmul(a_ref_val, s, half):
    return a_ref_val[0:1, :] * s + a_ref_val[1:2, :] * pltpu.roll(s, half, axis=1)


def _class_perm(tr, cr, ncls, transpose):
    a = lax.broadcasted_iota(jnp.int32, (tr, tr), 1 if transpose else 0)
    b = lax.broadcasted_iota(jnp.int32, (tr, tr), 0 if transpose else 1)
    return (b == _irem(a, cr) * ncls + _idiv(a, cr)).astype(BF16)


def _ssm_kernel(h_ref, kj_ref, ws_ref, wo_ref, apow_ref, s0_ref, y_ref, sf_ref, u_scr, yc_scr, sin_scr, kb_scr,
                *, ncls, nb, nc, half, unit_lanes, cr):
    r = nb * nc
    n2 = 2 * half
    nsteps = nc.bit_length() - 1
    nunit = SLAB // unit_lanes
    gpu = SLAB // n2
    gc = unit_lanes // gpu
    tr = ncls * cr
    ntile = r // cr
    row = lax.broadcasted_iota(jnp.int32, (r, n2), 0)
    chunk = row & (nc - 1)
    wr = lax.broadcasted_iota(jnp.int32, (SLAB, SLAB), 0)
    wc = lax.broadcasted_iota(jnp.int32, (SLAB, SLAB), 1)
    wrow = _idiv(wr, unit_lanes)
    own_state = _irem(_idiv(wr, gc), gpu) == _idiv(wc, n2)
    same_group = _idiv(wr, gc) == _idiv(wc, gc)
    zero = jnp.zeros((), BF16)

    def unit_op(ref_val, keep):
        w = ref_val if gpu == 1 else jnp.concatenate([ref_val] * gpu, axis=1)
        return jnp.where(keep, w, zero)

    for j in range(ncls):
        kb_scr[j] = jnp.where(same_group, jnp.concatenate([kj_ref[0, j]] * (SLAB // gc), axis=0), zero)

    perm = _class_perm(tr, cr, ncls, False)
    for i in range(ntile):
        uc = _dot(perm, h_ref[i * tr:(i + 1) * tr, :]).astype(BF16)
        for k in range(ncls):
            u_scr[k, i * cr:(i + 1) * cr, :] = uc[k * cr:(k + 1) * cr, :]

    def carry_states(p, apow, s0):
        s0_rows = jnp.zeros((r, n2), F32)
        for b in range(nb):
            s0_rows = jnp.where(row == b * nc, s0[b:b + 1, :], s0_rows)
        p = p + _cmul(apow[0], s0_rows, half)
        for k in range(nsteps):
            d = 1 << k
            shifted = jnp.where(chunk >= d, pltpu.roll(p, d, axis=0), 0.0)
            p = p + _cmul(apow[k], shifted, half)
        s_in = s0_rows if nc == 1 else jnp.where(chunk == 0, s0_rows, pltpu.roll(p, 1, axis=0))
        return s_in, p

    def unit_body(ui, carry):
        keep = (wrow == ui) & own_state
        sl = None
        for k in range(ncls):
            t = _dot(u_scr[k], unit_op(ws_ref[0, ncls - 1 - k], keep))
            sl = t if sl is None else sl + t
        parts = []
        for hf in range(gpu):
            g = ui * gpu + hf
            s_in, p = carry_states(sl[:, hf * n2:(hf + 1) * n2], apow_ref[g], s0_ref[g, 0])
            parts.append(s_in.astype(BF16))
            for b in range(nb):
                last = b * nc + nc - 1
                sf_ref[g, 0, b:b + 1, :] = p[last:last + 1, :]
        sin_scr[ui] = parts[0] if gpu == 1 else jnp.concatenate(parts, axis=1)
        return carry

    lax.fori_loop(0, nunit, unit_body, 0)

    keeps = [(wrow == ui) & own_state for ui in range(nunit)]
    for t in range(ncls):
        acc = None
        for k in range(t + 1):
            d = _dot(u_scr[k], kb_scr[t - k])
            acc = d if acc is None else acc + d
        for ui in range(nunit):
            acc = acc + _dot_nt(sin_scr[ui], unit_op(wo_ref[0, t], keeps[ui]))
        yc_scr[t] = acc

    perm_t = _class_perm(tr, cr, ncls, True)
    for i in range(ntile):
        yc = jnp.concatenate([yc_scr[k, i * cr:(i + 1) * cr, :] for k in range(ncls)], axis=0)
        hi = yc.astype(BF16)
        lo = (yc - hi.astype(F32)).astype(BF16)
        y_ref[i * tr:(i + 1) * tr, :] = _dot(perm_t, hi) + _dot(perm_t, lo)


def _ssm_call(h, kj, ws, wo, apow, s0, nbatch, ncls):
    m, d = h.shape
    t = m // nbatch
    nc = t // ncls
    assert nc & (nc - 1) == 0
    g, ns, _, n2 = apow.shape
    nslab = d // SLAB
    gps = g // nslab
    gc = SLAB // gps
    unit_lanes = (SLAB // n2) * gc
    bpb = max(b for b in range(1, nbatch + 1) if nbatch % b == 0 and b * nc <= S5_BLOCK_CHUNKS)
    nblk = nbatch // bpb
    r = bpb * nc
    cr = min(r, 32)
    kern = functools.partial(_ssm_kernel, ncls=ncls, nb=bpb, nc=nc, half=n2 // 2, unit_lanes=unit_lanes, cr=cr)
    st = pl.BlockSpec((gps, 1, bpb, n2), lambda i, j: (i, j, 0, 0))
    y, sf = pl.pallas_call(
        kern,
        grid=(nslab, nblk),
        in_specs=[pl.BlockSpec((bpb * t, SLAB), lambda i, j: (j, i)),
                  pl.BlockSpec((1, ncls, gc, SLAB), lambda i, j: (i, 0, 0, 0)),
                  pl.BlockSpec((1, ncls, SLAB, n2), lambda i, j: (i, 0, 0, 0)),
                  pl.BlockSpec((1, ncls, SLAB, n2), lambda i, j: (i, 0, 0, 0)),
                  pl.BlockSpec((gps, ns, 2, n2), lambda i, j: (i, 0, 0, 0)),
                  st],
        out_specs=[pl.BlockSpec((bpb * t, SLAB), lambda i, j: (j, i)), st],
        out_shape=[jax.ShapeDtypeStruct((m, d), F32),
                   jax.ShapeDtypeStruct((g, nblk, bpb, n2), F32)],
        scratch_shapes=[pltpu.VMEM((ncls, r, SLAB), BF16), pltpu.VMEM((ncls, r, SLAB), F32),
                        pltpu.VMEM((SLAB // unit_lanes, r, SLAB), BF16), pltpu.VMEM((ncls, SLAB, SLAB), BF16)],
        compiler_params=_params("parallel", "parallel"),
        name="ssm",
    )(h, kj, ws, wo, apow, s0.reshape(g, nblk, bpb, n2))
    return y, sf.reshape(g, nbatch, n2)


def _ssm_tables(lam_re, lam_im, log_step, b_re, b_im, c_re, c_im):
    L = SSM_CHUNK
    g, n = lam_re.shape
    gc = b_re.shape[2]
    lr, li = lam_re.astype(F32), lam_im.astype(F32)
    dt = jnp.exp(log_step.astype(F32))[:, None]
    mag = jnp.exp(lr * dt)
    ar, ai = mag * jnp.cos(li * dt), mag * jnp.sin(li * dt)
    den = lr * lr + li * li
    xr = ar - 1.0
    f_re = (xr * lr + ai * li) / den
    f_im = (ai * lr - xr * li) / den
    b_ret, b_imt = jnp.swapaxes(b_re, 1, 2), jnp.swapaxes(b_im, 1, 2)
    bb_re = f_re[:, None, :] * b_ret - f_im[:, None, :] * b_imt
    bb_im = f_re[:, None, :] * b_imt + f_im[:, None, :] * b_ret
    j = jnp.arange(L + 1, dtype=F32)[:, None, None]
    pmag = jnp.exp(j * (lr * dt))
    pw_re, pw_im = pmag * jnp.cos(j * (li * dt)), pmag * jnp.sin(j * (li * dt))
    pb_re = pw_re[:, :, None, :] * bb_re - pw_im[:, :, None, :] * bb_im
    pb_im = pw_re[:, :, None, :] * bb_im + pw_im[:, :, None, :] * bb_re
    nslab = g * gc // SLAB
    n2 = 2 * n
    kj = (jnp.einsum('gcn,jgdn->jdgc', c_re, pb_re[:L], precision=HI)
          - jnp.einsum('gcn,jgdn->jdgc', c_im, pb_im[:L], precision=HI))
    kj = jnp.transpose(kj.reshape(L, gc, nslab, SLAB), (2, 0, 1, 3))

    def per_slab(w):
        return jnp.transpose(w.reshape(L, nslab, SLAB, n2), (1, 0, 2, 3))

    ws = per_slab(jnp.concatenate([pb_re[:L], pb_im[:L]], axis=3))
    cp_re = c_re[None] * pw_re[1:, :, None, :] - c_im[None] * pw_im[1:, :, None, :]
    cp_im = c_re[None] * pw_im[1:, :, None, :] + c_im[None] * pw_re[1:, :, None, :]
    wo = per_slab(jnp.concatenate([cp_re, -cp_im], axis=3))
    return kj.astype(BF16), ws.astype(BF16), wo.astype(BF16), pw_re, pw_im


def _ssm_state_powers(a_re, a_im, nc):
    steps = []
    for _ in range(max(nc.bit_length() - 1, 1)):
        steps.append(jnp.stack([jnp.concatenate([a_re, a_re], -1), jnp.concatenate([-a_im, a_im], -1)], axis=1))
        a_re, a_im = a_re * a_re - a_im * a_im, 2.0 * a_re * a_im
    return jnp.stack(steps, axis=1)


def _glu_kernel(x_ref, y_ref, nw_ref, sh_ref, sc_ref, d_ref, wv_ref, wg_ref, xr_ref, g_ref, o_ref, a_scr):
    @pl.when(pl.program_id(1) == 0)
    def _():
        for r0 in range(0, x_ref.shape[0], PROLOGUE_ROWS):
            rs = min(PROLOGUE_ROWS, x_ref.shape[0])
            h = _norm_mod(x_ref[r0:r0 + rs, :], nw_ref[...], _mod_rows(sh_ref, r0, rs), _mod_rows(sc_ref, r0, rs))
            a_scr[r0:r0 + rs, :] = jax.nn.gelu(y_ref[r0:r0 + rs, :] + d_ref[...] * h).astype(BF16)

    a = a_scr[...]
    val = _dot(a, wv_ref[...])
    gate = _dot(a, wg_ref[...])
    o_ref[...] = xr_ref[...] + g_ref[0] * (val / (1.0 + jnp.exp(-gate)))


def _glu_call(x, y, nw, sh, sc, dskip, w, g1, nb):
    m, d = x.shape
    rows = _Rows(m, nb, 512)
    tm, r = rows.tm, sh.shape[1]
    tn = _tile(d, 512)
    nn = d // tn
    return pl.pallas_call(
        _glu_kernel,
        grid=(rows.nt, nn),
        in_specs=[pl.BlockSpec((tm, d), lambda i, j: (i, 0)),
                  pl.BlockSpec((tm, d), lambda i, j: (i, 0)),
                  pl.BlockSpec((1, d), lambda i, j: (0, 0)),
                  rows.mod_spec(r, d), rows.mod_spec(r, d),
                  pl.BlockSpec((1, d), lambda i, j: (0, 0)),
                  pl.BlockSpec((d, tn), lambda i, j: (0, j)),
                  pl.BlockSpec((d, tn), lambda i, j: (0, j + nn)),
                  pl.BlockSpec((tm, tn), lambda i, j: (i, j)),
                  rows.mod_spec(r, tn, col=True)],
        out_specs=pl.BlockSpec((tm, tn), lambda i, j: (i, j)),
        out_shape=jax.ShapeDtypeStruct((m, d), F32),
        scratch_shapes=[pltpu.VMEM((tm, d), BF16)],
        compiler_params=_params("parallel", "arbitrary"),
        name="glu",
    )(x, y, nw, sh, sc, dskip, w, w, x, g1)


def _mlp_kernel(x_ref, nw_ref, sh_ref, sc_ref, g_ref, up_ref, dn_ref, o_ref, h_scr, *, nf):
    f = pl.program_id(1)

    @pl.when(f == 0)
    def _():
        for r0 in range(0, x_ref.shape[0], PROLOGUE_ROWS):
            rs = min(PROLOGUE_ROWS, x_ref.shape[0])
            h = _norm_mod(x_ref[r0:r0 + rs, :], nw_ref[...], _mod_rows(sh_ref, r0, rs), _mod_rows(sc_ref, r0, rs))
            h_scr[r0:r0 + rs, :] = h.astype(BF16)
        o_ref[...] = jnp.zeros(o_ref.shape, F32)

    a = jnp.maximum(_dot(h_scr[...], up_ref[...]), 0.0)
    o_ref[...] += _dot((a * a).astype(BF16), dn_ref[...])

    @pl.when(f == nf - 1)
    def _():
        o_ref[...] = x_ref[...] + g_ref[0] * o_ref[...]


def _mlp_call(x, nw, sh, sc, g2, up, dn, nb):
    m, d = x.shape
    ff = up.shape[1]
    rows = _Rows(m, nb, 512)
    tm, r = rows.tm, sh.shape[1]
    tf = _tile(ff, 1024)
    nf = ff // tf
    return pl.pallas_call(
        functools.partial(_mlp_kernel, nf=nf),
        grid=(rows.nt, nf),
        in_specs=[pl.BlockSpec((tm, d), lambda i, j: (i, 0)),
                  pl.BlockSpec((1, d), lambda i, j: (0, 0)),
                  rows.mod_spec(r, d), rows.mod_spec(r, d), rows.mod_spec(r, d),
                  pl.BlockSpec((d, tf), lambda i, j: (0, j)),
                  pl.BlockSpec((tf, d), lambda i, j: (j, 0))],
        out_specs=pl.BlockSpec((tm, d), lambda i, j: (i, 0)),
        out_shape=jax.ShapeDtypeStruct((m, d), F32),
        scratch_shapes=[pltpu.VMEM((tm, d), BF16)],
        compiler_params=_params("parallel", "arbitrary"),
        name="mlp",
    )(x, nw, sh, sc, g2, up, dn)


def _mod_rows(ref, r0, ts):
    return ref[0] if ref.shape[1] == 1 else ref[0, r0:r0 + ts, :]


def _qproj_kernel(x_ref, nw_ref, sh_ref, sc_ref, w_ref, cw_ref, q_ref, *, hd, ts, qscale):
    cw = cw_ref[...] * qscale
    for r0 in range(0, x_ref.shape[0], ts):
        h = _norm_mod(x_ref[r0:r0 + ts, :], nw_ref[...], _mod_rows(sh_ref, r0, ts), _mod_rows(sc_ref, r0, ts))
        q = _dot(h.astype(BF16), w_ref[...])
        for c in range(q.shape[1] // hd):
            q_ref[r0:r0 + ts, c * hd:(c + 1) * hd] = _rms(q[:, c * hd:(c + 1) * hd], cw).astype(BF16)


def _qproj_call(x, nw, sh, sc, wq, qnw, nb, qscale):
    m, d = x.shape
    n = wq.shape[1]
    hd = qnw.shape[1]
    rows = _Rows(m, nb, 512)
    tm, r = rows.tm, sh.shape[1]
    tpn = rows.tiles_per_nb
    mod = pl.BlockSpec((1, r, d), lambda i: (i // tpn, 0, 0))
    return pl.pallas_call(
        functools.partial(_qproj_kernel, hd=hd, ts=min(tm, 256), qscale=qscale),
        grid=(rows.nt,),
        in_specs=[pl.BlockSpec((tm, d), lambda i: (i, 0)),
                  pl.BlockSpec((1, d), lambda i: (0, 0)),
                  mod, mod,
                  pl.BlockSpec((d, n), lambda i: (0, 0)),
                  pl.BlockSpec((1, hd), lambda i: (0, 0))],
        out_specs=pl.BlockSpec((tm, n), lambda i: (i, 0)),
        out_shape=jax.ShapeDtypeStruct((m, n), BF16),
        compiler_params=_params("parallel"),
        name="q_proj",
    )(x, nw, sh, sc, wq, qnw)


def _kvproj_kernel(x_ref, nw_ref, sh_ref, sc_ref, wk_ref, wv_ref, cw_ref, k_ref, v_ref, kb_ref, vb_ref, h_scr,
                   *, hd):
    @pl.when(pl.program_id(1) == 0)
    def _():
        for r0 in range(0, x_ref.shape[0], PROLOGUE_ROWS):
            rs = min(PROLOGUE_ROWS, x_ref.shape[0])
            h = _norm_mod(x_ref[r0:r0 + rs, :], nw_ref[...], _mod_rows(sh_ref, r0, rs), _mod_rows(sc_ref, r0, rs))
            h_scr[r0:r0 + rs, :] = h.astype(BF16)

    h = h_scr[...]
    k = _chunk_rms(_dot(h, wk_ref[...]), cw_ref[...], hd)
    v = _dot(h, wv_ref[...])
    k_ref[...] = k
    v_ref[...] = v
    kb_ref[...] = k.astype(BF16)
    vb_ref[...] = v.astype(BF16)


def _kvproj_call(x, nw, sh, sc, w, knw, nb):
    m, d = x.shape
    nk = w.shape[1] // 2
    hd = knw.shape[1]
    rows = _Rows(m, nb, 512)
    tm, r = rows.tm, sh.shape[1]
    tn = _tile(nk, 512)
    nn = nk // tn
    out_spec = pl.BlockSpec((tm, tn), lambda i, j: (i, j))
    return pl.pallas_call(
        functools.partial(_kvproj_kernel, hd=hd),
        grid=(rows.nt, nn),
        in_specs=[pl.BlockSpec((tm, d), lambda i, j: (i, 0)),
                  pl.BlockSpec((1, d), lambda i, j: (0, 0)),
                  rows.mod_spec(r, d), rows.mod_spec(r, d),
                  pl.BlockSpec((d, tn), lambda i, j: (0, j)),
                  pl.BlockSpec((d, tn), lambda i, j: (0, j + nn)),
                  pl.BlockSpec((1, hd), lambda i, j: (0, 0))],
        out_specs=[out_spec, out_spec, out_spec, out_spec],
        out_shape=[jax.ShapeDtypeStruct((m, nk), F32), jax.ShapeDtypeStruct((m, nk), F32),
                   jax.ShapeDtypeStruct((m, nk), BF16), jax.ShapeDtypeStruct((m, nk), BF16)],
        scratch_shapes=[pltpu.VMEM((tm, d), BF16)],
        compiler_params=_params("parallel", "arbitrary"),
        name="kv_proj",
    )(x, nw, sh, sc, w, w, knw)


def _oproj_kernel(o_ref, w_ref, x_ref, g_ref, y_ref):
    y_ref[...] = x_ref[...] + g_ref[0] * _dot(o_ref[...], w_ref[...])


def _oproj_call(o, wo, x, g1, nb):
    m, d = x.shape
    rows = _Rows(m, nb, 512)
    tm, r = rows.tm, g1.shape[1]
    tpn = rows.tiles_per_nb
    return pl.pallas_call(
        _oproj_kernel,
        grid=(rows.nt,),
        in_specs=[pl.BlockSpec((tm, o.shape[1]), lambda i: (i, 0)),
                  pl.BlockSpec((o.shape[1], d), lambda i: (0, 0)),
                  pl.BlockSpec((tm, d), lambda i: (i, 0)),
                  pl.BlockSpec((1, r, d), lambda i: (i // tpn, 0, 0))],
        out_specs=pl.BlockSpec((tm, d), lambda i: (i, 0)),
        out_shape=jax.ShapeDtypeStruct((m, d), F32),
        compiler_params=_params("parallel"),
        name="o_proj",
    )(o, wo, x, g1)


def _lambda(lq1, lk1, lq2, lk2, lam_init):
    return (jnp.exp(jnp.sum(lq1 * lk1, axis=-1, keepdims=True))
            - jnp.exp(jnp.sum(lq2 * lk2, axis=-1, keepdims=True)) + lam_init)


def _pattn_kernel(q_ref, k_ref, v_ref, lq1_ref, lk1_ref, lq2_ref, lk2_ref, swc_ref, o_ref, m_scr, l_scr, acc_scr,
                  *, hd, tq, lam_init):
    qi = pl.program_id(2)
    qt = q_ref[...].astype(F32).T
    top = lax.broadcasted_iota(jnp.int32, qt.shape, 0) < hd
    qw = jnp.concatenate([jnp.where(top, qt, 0.0), jnp.where(top, 0.0, qt)], axis=1).astype(BF16)
    m_scr[...] = jnp.full(m_scr.shape, -jnp.inf, F32)
    l_scr[...] = jnp.zeros(l_scr.shape, F32)
    acc_scr[...] = jnp.zeros(acc_scr.shape, F32)

    def block(kstart, masked):
        k = k_ref[pl.ds(kstart, tq), :]
        v = v_ref[pl.ds(kstart, tq), :]
        s = _dot(k, qw)
        if masked:
            key = lax.broadcasted_iota(jnp.int32, s.shape, 0)
            qry = _irem(lax.broadcasted_iota(jnp.int32, s.shape, 1), tq)
            s = jnp.where(key <= qry, s, -jnp.inf)
        m_prev = m_scr[...]
        m_new = jnp.maximum(m_prev, jnp.max(s, axis=0, keepdims=True))
        alpha = jnp.exp2(m_prev - m_new)
        p = jnp.exp2(s - m_new)
        l_scr[...] = alpha * l_scr[...] + jnp.sum(p, axis=0, keepdims=True)
        pv = lax.dot_general(v, p.astype(BF16), (((0,), (0,)), ((), ())), preferred_element_type=F32)
        acc_scr[...] = alpha * acc_scr[...] + pv
        m_scr[...] = m_new

    def body(j, carry):
        block(pl.multiple_of(j * tq, tq), False)
        return carry

    lax.fori_loop(0, qi, body, 0)
    block(pl.multiple_of(qi * tq, tq), True)
    lam = _lambda(lq1_ref[...], lk1_ref[...], lq2_ref[...], lk2_ref[...], lam_init)
    on = acc_scr[...] / l_scr[...]
    o = on[:, :tq] - lam * on[:, tq:]
    ms = jnp.mean(o * o, axis=0, keepdims=True)
    o = o * lax.rsqrt(ms + NORM_EPS) * swc_ref[...] * (1.0 - lam_init)
    o_ref[...] = o.T.astype(BF16)


def _pattn_call(q, kb, vb, lq1, lk1, lq2, lk2, subw, nbatch, lam_init):
    m, d = q.shape
    vd = subw.shape[1]
    hd = lq1.shape[1]
    nh = d // vd
    t = m // nbatch
    tq = _tile(t, ATTN_TILE)
    nq = t // tq
    kern = functools.partial(_pattn_kernel, hd=hd, tq=tq, lam_init=lam_init)
    vec = pl.BlockSpec((1, hd), lambda b, h, i: (0, 0))
    return pl.pallas_call(
        kern,
        grid=(nbatch, nh, nq),
        in_specs=[pl.BlockSpec((tq, vd), lambda b, h, i: (b * nq + i, h)),
                  pl.BlockSpec((t, vd), lambda b, h, i: (b, h)),
                  pl.BlockSpec((t, vd), lambda b, h, i: (b, h)),
                  vec, vec, vec, vec,
                  pl.BlockSpec((vd, 1), lambda b, h, i: (0, 0))],
        out_specs=pl.BlockSpec((tq, vd), lambda b, h, i: (b * nq + i, h)),
        out_shape=jax.ShapeDtypeStruct((m, d), BF16),
        scratch_shapes=[pltpu.VMEM((1, 2 * tq), F32), pltpu.VMEM((1, 2 * tq), F32), pltpu.VMEM((vd, 2 * tq), F32)],
        compiler_params=_params("parallel", "parallel", "parallel"),
        name="prompt_attn",
    )(q, kb, vb, lq1, lk1, lq2, lk2, subw.reshape(vd, 1))


def _idiv(x, n):
    if n & (n - 1) == 0:
        return lax.shift_right_logical(x, jnp.full_like(x, n.bit_length() - 1))
    return lax.div(x, jnp.full_like(x, n))


def _irem(x, n):
    if n & (n - 1) == 0:
        return x & (n - 1)
    return lax.rem(x, jnp.full_like(x, n))


def _sattn_kernel(pt_ref, qw_ref, *refs, npg, nh, nq, ncol, lam_init):
    kp_refs = refs[:npg]
    vp_refs = refs[npg:2 * npg]
    kn_ref, vn_ref, lq1_ref, lk1_ref, lq2_ref, lk2_ref, swc_ref, o_ref, m_scr, l_scr, acc_scr = refs[2 * npg:]
    s_id = pl.program_id(1)
    qw = qw_ref[0]
    jp = qw.shape[1]

    @pl.when(s_id == 0)
    def _():
        m_scr[...] = jnp.full(m_scr.shape, -jnp.inf, F32)
        l_scr[...] = jnp.zeros(l_scr.shape, F32)
        acc_scr[...] = jnp.zeros(acc_scr.shape, F32)

    def same_head(nrows):
        row = lax.broadcasted_iota(jnp.int32, (nrows, jp), 0)
        col = lax.broadcasted_iota(jnp.int32, (nrows, jp), 1)
        return row, col, (_irem(row, nh) == _idiv(col, 2 * nq)) | (col >= ncol)

    def update(k, v, keep):
        s = jnp.where(keep, _dot(k, qw), -jnp.inf)
        m_prev = m_scr[...]
        m_new = jnp.maximum(m_prev, jnp.max(s, axis=0, keepdims=True))
        alpha = jnp.exp2(m_prev - m_new)
        p = jnp.exp2(s - m_new)
        l_scr[...] = alpha * l_scr[...] + jnp.sum(p, axis=0, keepdims=True)
        pv = lax.dot_general(v, p.astype(BF16), (((0,), (0,)), ((), ())), preferred_element_type=F32)
        acc_scr[...] = alpha * acc_scr[...] + pv
        m_scr[...] = m_new

    _, _, keep_page = same_head(kp_refs[0].shape[1])
    for j in range(npg):
        update(kp_refs[j][0].astype(BF16), vp_refs[j][0].astype(BF16), keep_page)

    @pl.when(s_id == pl.num_programs(1) - 1)
    def _():
        row, col, keep = same_head(kn_ref.shape[1])
        update(kn_ref[0], vn_ref[0], keep & ((_idiv(row, nh) <= _irem(col, nq)) | (col >= ncol)))
        lam = _lambda(lq1_ref[...], lk1_ref[...], lq2_ref[...], lk2_ref[...], lam_init)
        on = acc_scr[...] / l_scr[...]
        o = on - lam * pltpu.roll(on, jp - nq, axis=1)
        ms = jnp.mean(o * o, axis=0, keepdims=True)
        o_ref[0] = o * lax.rsqrt(ms + NORM_EPS) * swc_ref[...] * (1.0 - lam_init)


def _sattn_call(page_table, qw, cache_k, cache_v, k_new, v_new, lq1, lk1, lq2, lk2, subw_col, nh, nq, ncol,
                lam_init):
    nbatch, kd, jp = qw.shape
    prow = cache_k.shape[1]
    vd = cache_v.shape[2]
    hd = lq1.shape[1]
    npages = page_table.shape[1]
    npg = _tile(npages, 8)
    nk = k_new.shape[1]
    kern = functools.partial(_sattn_kernel, npg=npg, nh=nh, nq=nq, ncol=ncol, lam_init=lam_init)

    def page_spec(j, width):
        return pl.BlockSpec((1, prow, width), lambda b, s, pt: (pt[b, s * npg + j], 0, 0))

    vec = pl.BlockSpec((1, hd), lambda b, s, pt: (0, 0))
    grid_spec = pltpu.PrefetchScalarGridSpec(
        num_scalar_prefetch=1,
        grid=(nbatch, npages // npg),
        in_specs=([pl.BlockSpec((1, kd, jp), lambda b, s, pt: (b, 0, 0))]
                  + [page_spec(j, kd) for j in range(npg)] + [page_spec(j, vd) for j in range(npg)]
                  + [pl.BlockSpec((1, nk, kd), lambda b, s, pt: (b, 0, 0)),
                     pl.BlockSpec((1, nk, vd), lambda b, s, pt: (b, 0, 0)),
                     vec, vec, vec, vec,
                     pl.BlockSpec((vd, 1), lambda b, s, pt: (0, 0))]),
        out_specs=pl.BlockSpec((1, vd, jp), lambda b, s, pt: (b, 0, 0)),
        scratch_shapes=[pltpu.VMEM((1, jp), F32), pltpu.VMEM((1, jp), F32), pltpu.VMEM((vd, jp), F32)],
    )
    return pl.pallas_call(
        kern,
        grid_spec=grid_spec,
        out_shape=jax.ShapeDtypeStruct((nbatch, vd, jp), F32),
        compiler_params=_params("parallel", "arbitrary"),
        name="sample_attn",
    )(page_table, qw, *([cache_k] * npg), *([cache_v] * npg), k_new, v_new, lq1, lk1, lq2, lk2, subw_col)


def _run_group(x3, mods, kvmod, s_re0, s_im0, attend, p, nb):
    nbatch, t, d = x3.shape
    m = nbatch * t
    x = x3.reshape(m, d)
    depth = p['norm_w'].shape[0]
    n_a = p['lam_re'].shape[0]
    g, n = p['lam_re'].shape[1:]
    tv = min(SSM_CHUNK, t)
    assert t % tv == 0
    nc = t // tv
    new_re, new_im = [], []
    k = v = kb = vb = None
    for l in range(depth):
        sh1, sc1, g1, sh2, sc2, g2 = mods[l]
        nw1 = p['norm_w'][l, 0][None]
        if l < n_a:
            kj, ws, wo, pw_re, pw_im = p['ssm_tabs'][l]
            apow = _ssm_state_powers(pw_re[tv], pw_im[tv], nc)
            h = _norm_mod_call(x, nw1, sh1, sc1, nb)
            s0 = jnp.concatenate([jnp.transpose(s_re0[l], (1, 0, 2)), jnp.transpose(s_im0[l], (1, 0, 2))], -1)
            y, sf = _ssm_call(h, kj, ws, wo, apow, s0.astype(F32), nbatch, tv)
            new_re.append(jnp.transpose(sf[..., :n], (1, 0, 2)))
            new_im.append(jnp.transpose(sf[..., n:], (1, 0, 2)))
            x = _glu_call(x, y, nw1, sh1, sc1, p['ssm_d'][l][None], p['glu_w'][l], g1, nb)
        else:
            j = l - n_a
            lam_init = 0.8 - 0.6 * math.exp(-0.3 * l)
            hd = p['q_norm_w'].shape[1]
            q = _qproj_call(x, nw1, sh1, sc1, p['attn_wq'][j], p['q_norm_w'][j][None], nb,
                            hd ** -0.5 * math.log2(math.e))
            o = attend(q, kb, vb, p['lambda_q1'][j][None], p['lambda_k1'][j][None], p['lambda_q2'][j][None],
                       p['lambda_k2'][j][None], p['subln_w'][j][None], lam_init)
            x = _oproj_call(o, p['attn_wo'][j], x, g1, nb)
        x = _mlp_call(x, p['norm_w'][l, 1][None], sh2, sc2, g2, p['mlp_up'][l], p['mlp_down'][l], nb)
        if l == n_a - 1:
            k, v, kb, vb = _kvproj_call(x, p['kv_norm_w'][None], kvmod[0], kvmod[1], p['kv_w'], p['k_norm_w'][None],
                                        nb)
    return x.reshape(nbatch, t, d), jnp.stack(new_re), jnp.stack(new_im), k, v


def kernel(x_prompt, x_sample, state_ssm_re, state_ssm_im, cache_k, cache_v, page_table, c_prompt, c_sample, ada_w, ada_b, norm_w, mlp_up, mlp_down, ssm_lambda_re, ssm_lambda_im, ssm_log_step, ssm_b_re, ssm_b_im, ssm_c_re, ssm_c_im, ssm_d, glu_w, kv_ada_w, kv_ada_b, kv_norm_w, kv_w, k_norm_w, attn_wq, q_norm_w, lambda_q1, lambda_k1, lambda_q2, lambda_k2, subln_w, attn_wo):
    bp, tp, d = x_prompt.shape
    bs, ts, _ = x_sample.shape
    depth = ada_w.shape[0]
    n_a = ssm_lambda_re.shape[0]
    nh, vd = cache_v.shape[2], cache_v.shape[3]
    hd = q_norm_w.shape[1]

    p = {
        'norm_w': norm_w, 'mlp_up': mlp_up.astype(BF16), 'mlp_down': mlp_down.astype(BF16),
        'lam_re': ssm_lambda_re,
        'ssm_tabs': [_ssm_tables(ssm_lambda_re[l], ssm_lambda_im[l], ssm_log_step[l], ssm_b_re[l].astype(F32),
                                 ssm_b_im[l].astype(F32), ssm_c_re[l].astype(F32), ssm_c_im[l].astype(F32))
                     for l in range(n_a)],
        'ssm_d': ssm_d, 'glu_w': glu_w.astype(BF16), 'kv_norm_w': kv_norm_w, 'kv_w': kv_w.astype(BF16),
        'k_norm_w': k_norm_w, 'attn_wq': attn_wq.astype(BF16), 'q_norm_w': q_norm_w,
        'lambda_q1': lambda_q1, 'lambda_k1': lambda_k1, 'lambda_q2': lambda_q2, 'lambda_k2': lambda_k2,
        'subln_w': subln_w, 'attn_wo': attn_wo.astype(BF16),
    }

    nc_rows = bp + bs
    c_all = jnp.concatenate([c_prompt, c_sample], axis=0)
    c_all = jnp.pad(c_all, ((0, -nc_rows % 16), (0, 0))).astype(BF16)
    mod_all = _ada(c_all, ada_w, ada_b[:, None, :])
    kvmod_all = _ada(c_all, kv_ada_w[None], kv_ada_b[None, None, :])[0]

    def split_mods(lo, hi, rep):
        def shape(a):
            return a[:, None, :] if rep == 1 else jnp.repeat(a, rep, axis=0)[None]
        mods = [[shape(mod_all[l, lo:hi, i * d:(i + 1) * d]) for i in range(6)] for l in range(depth)]
        kvmod = [shape(kvmod_all[lo:hi, i * d:(i + 1) * d]) for i in range(2)]
        return mods, kvmod

    mods_p, kvmod_p = split_mods(0, bp, 1)
    zeros = jnp.zeros((n_a, bp) + ssm_lambda_re.shape[1:], F32)

    def attend_prompt(q, kb, vb, lq1, lk1, lq2, lk2, subw, lam_init):
        return _pattn_call(q, kb, vb, lq1, lk1, lq2, lk2, subw, bp, lam_init)

    y_p, re_p, im_p, k_p, v_p = _run_group(x_prompt, mods_p, kvmod_p, zeros, zeros, attend_prompt, p, bp)

    mods_s, kvmod_s = split_mods(bp, bp + bs, ts)
    page = cache_k.shape[1]
    ck = cache_k.reshape(cache_k.shape[0], page * nh, 2 * hd)
    cv = cache_v.reshape(cache_v.shape[0], page * nh, vd)
    eye2 = jnp.eye(2, dtype=BF16)
    ncol = nh * 2 * ts
    jp = -(-ncol // 128) * 128

    def attend_sample(q, kb, vb, lq1, lk1, lq2, lk2, subw, lam_init):
        q5 = q.reshape(bs, ts, nh, 2, hd)
        qw = jnp.einsum('bihmd,mM->bmdhMi', q5, eye2).reshape(bs, 2 * hd, ncol)
        qw = jnp.pad(qw, ((0, 0), (0, 0), (0, jp - ncol)))
        k_new = kb.reshape(bs, ts * nh, 2 * hd)
        v_new = vb.reshape(bs, ts * nh, vd)
        ot = _sattn_call(page_table, qw, ck, cv, k_new, v_new, lq1, lk1, lq2, lk2, subw.reshape(vd, 1),
                         nh, ts, ncol, lam_init)
        o = ot[:, :, :ncol].reshape(bs, vd, nh, 2, ts)[:, :, :, 0, :]
        return jnp.transpose(o, (0, 3, 2, 1)).reshape(bs * ts, d).astype(BF16)

    y_s, re_s, im_s, k_s, v_s = _run_group(x_sample, mods_s, kvmod_s, state_ssm_re, state_ssm_im, attend_sample,
                                           p, 1)

    return (y_p, y_s, re_p, im_p,
            k_p.reshape(bp, tp, nh, 2 * hd), v_p.reshape(bp, tp, nh, vd),
            re_s, im_s,
            k_s.reshape(bs, ts, nh, 2 * hd), v_s.reshape(bs, ts, nh, vd))
```

```python
import functools
import math

import jax
import jax.numpy as jnp
from jax import lax
from jax.experimental import pallas as pl
from jax.experimental.pallas import tpu as pltpu

F32 = jnp.float32
BF16 = jnp.bfloat16
HI = lax.Precision.HIGHEST

NORM_EPS = 1e-6
SSM_CHUNK = 16
SLAB = 256
S5_BLOCK_CHUNKS = 256
S5_PERM_CHUNKS = 32
PROLOGUE_ROWS = 16
MIB = 1024 * 1024
VMEM_LIMIT_BYTES = 52 * MIB
ATTN_TILE = 512


def _params(*sem):
    return pltpu.CompilerParams(dimension_semantics=sem, vmem_limit_bytes=VMEM_LIMIT_BYTES)


def _tile(dim, pref):
    t = min(dim, pref)
    assert dim % t == 0, (dim, pref)
    return t


def _dot(a, b):
    return jnp.dot(a, b, preferred_element_type=F32)


def _dot_nt(a, b):
    return lax.dot_general(a, b, (((1,), (1,)), ((), ())), preferred_element_type=F32)


def _rms(x, w):
    return x * lax.rsqrt(jnp.mean(x * x, axis=-1, keepdims=True) + NORM_EPS) * w


def _norm_mod(x, nw, sh, sc):
    return _rms(x, nw) * (1.0 + sc) + sh


def _chunk_rms(y, w, width):
    parts = []
    for c in range(y.shape[-1] // width):
        parts.append(_rms(y[:, c * width:(c + 1) * width], w))
    return parts[0] if len(parts) == 1 else jnp.concatenate(parts, axis=-1)


def _ada_kernel(c_ref, w_ref, b_ref, o_ref):
    o_ref[0] = _dot(c_ref[...], w_ref[0].astype(BF16)) + b_ref[0]


def _ada(c, w, b):
    nl, d, n = w.shape
    r = c.shape[0]
    tn = _tile(n, 512)
    return pl.pallas_call(
        _ada_kernel,
        grid=(nl, n // tn),
        in_specs=[pl.BlockSpec((r, d), lambda l, j: (0, 0)),
                  pl.BlockSpec((1, d, tn), lambda l, j: (l, 0, j)),
                  pl.BlockSpec((1, 1, tn), lambda l, j: (l, 0, j))],
        out_specs=pl.BlockSpec((1, r, tn), lambda l, j: (l, 0, j)),
        out_shape=jax.ShapeDtypeStruct((nl, r, n), F32),
        compiler_params=_params("parallel", "parallel"),
        name="ada",
    )(c, w, b)


class _Rows:
    def __init__(self, m, nb, tm_pref):
        self.m = m
        self.nb = nb
        rows_per_nb = m // nb
        self.tm = _tile(rows_per_nb, tm_pref)
        self.tiles_per_nb = rows_per_nb // self.tm
        self.nt = m // self.tm

    def mod_spec(self, r, width, col=None):
        tpn = self.tiles_per_nb
        if col is None:
            return pl.BlockSpec((1, r, width), lambda i, j: (i // tpn, 0, 0))
        return pl.BlockSpec((1, r, width), lambda i, j: (i // tpn, 0, j))


def _norm_mod_kernel(x_ref, nw_ref, sh_ref, sc_ref, o_ref):
    for r0 in range(0, x_ref.shape[0], PROLOGUE_ROWS):
        rs = min(PROLOGUE_ROWS, x_ref.shape[0])
        h = _norm_mod(x_ref[r0:r0 + rs, :], nw_ref[...], _mod_rows(sh_ref, r0, rs), _mod_rows(sc_ref, r0, rs))
        o_ref[r0:r0 + rs, :] = h.astype(BF16)


def _norm_mod_call(x, nw, sh, sc, nb):
    m, d = x.shape
    rows = _Rows(m, nb, 512)
    tm, r = rows.tm, sh.shape[1]
    tpn = rows.tiles_per_nb
    return pl.pallas_call(
        _norm_mod_kernel,
        grid=(rows.nt,),
        in_specs=[pl.BlockSpec((tm, d), lambda i: (i, 0)),
                  pl.BlockSpec((1, d), lambda i: (0, 0)),
                  pl.BlockSpec((1, r, d), lambda i: (i // tpn, 0, 0)),
                  pl.BlockSpec((1, r, d), lambda i: (i // tpn, 0, 0))],
        out_specs=pl.BlockSpec((tm, d), lambda i: (i, 0)),
        out_shape=jax.ShapeDtypeStruct((m, d), BF16),
        compiler_params=_params("parallel"),
        name="norm_mod",
    )(x, nw, sh, sc)


def _cmul(a_ref_val, s, half):
    return a_ref_val[0:1, :] * s + a_ref_val[1:2, :] * pltpu.roll(s, half, axis=1)


def _class_perm(tr, cr, ncls, transpose):
    a = lax.broadcasted_iota(jnp.int32, (tr, tr), 1 if transpose else 0)
    b = lax.broadcasted_iota(jnp.int32, (tr, tr), 0 if transpose else 1)
    return (b == _irem(a, cr) * ncls + _idiv(a, cr)).astype(BF16)


def _ssm_kernel(h_ref, kj_ref, ws_ref, wo_ref, apow_ref, s0_ref, y_ref, sf_ref, u_scr, yc_scr, sin_scr, kb_scr,
                *, ncls, nb, nc, half, unit_lanes, cr):
    r = nb * nc
    n2 = 2 * half
    nsteps = nc.bit_length() - 1
    nunit = SLAB // unit_lanes
    gpu = SLAB // n2
    gc = unit_lanes // gpu
    tr = ncls * cr
    ntile = r // cr
    row = lax.broadcasted_iota(jnp.int32, (r, n2), 0)
    chunk = row & (nc - 1)
    wr = lax.broadcasted_iota(jnp.int32, (SLAB, SLAB), 0)
    wc = lax.broadcasted_iota(jnp.int32, (SLAB, SLAB), 1)
    wrow = _idiv(wr, unit_lanes)
    own_state = _irem(_idiv(wr, gc), gpu) == _idiv(wc, n2)
    same_group = _idiv(wr, gc) == _idiv(wc, gc)
    zero = jnp.zeros((), BF16)

    def unit_op(ref_val, keep):
        w = ref_val if gpu == 1 else jnp.concatenate([ref_val] * gpu, axis=1)
        return jnp.where(keep, w, zero)

    for j in range(ncls):
        kb_scr[j] = jnp.where(same_group, jnp.concatenate([kj_ref[0, j]] * (SLAB // gc), axis=0), zero)

    perm = _class_perm(tr, cr, ncls, False)
    for i in range(ntile):
        uc = _dot(perm, h_ref[i * tr:(i + 1) * tr, :]).astype(BF16)
        for k in range(ncls):
            u_scr[k, i * cr:(i + 1) * cr, :] = uc[k * cr:(k + 1) * cr, :]

    def carry_states(p, apow, s0):
        s0_rows = jnp.zeros((r, n2), F32)
        for b in range(nb):
            s0_rows = jnp.where(row == b * nc, s0[b:b + 1, :], s0_rows)
        p = p + _cmul(apow[0], s0_rows, half)
        for k in range(nsteps):
            d = 1 << k
            shifted = jnp.where(chunk >= d, pltpu.roll(p, d, axis=0), 0.0)
            p = p + _cmul(apow[k], shifted, half)
        s_in = s0_rows if nc == 1 else jnp.where(chunk == 0, s0_rows, pltpu.roll(p, 1, axis=0))
        return s_in, p

    def unit_body(ui, carry):
        keep = (wrow == ui) & own_state
        sl = None
        for k in range(ncls):
            t = _dot(u_scr[k], unit_op(ws_ref[0, ncls - 1 - k], keep))
            sl = t if sl is None else sl + t
        parts = []
        for hf in range(gpu):
            g = ui * gpu + hf
            s_in, p = carry_states(sl[:, hf * n2:(hf + 1) * n2], apow_ref[g], s0_ref[g, 0])
            parts.append(s_in.astype(BF16))
            for b in range(nb):
                last = b * nc + nc - 1
                sf_ref[g, 0, b:b + 1, :] = p[last:last + 1, :]
        sin_scr[ui] = parts[0] if gpu == 1 else jnp.concatenate(parts, axis=1)
        return carry

    lax.fori_loop(0, nunit, unit_body, 0)

    keeps = [(wrow == ui) & own_state for ui in range(nunit)]
    for t in range(ncls):
        acc = None
        for k in range(t + 1):
            d = _dot(u_scr[k], kb_scr[t - k])
            acc = d if acc is None else acc + d
        for ui in range(nunit):
            acc = acc + _dot_nt(sin_scr[ui], unit_op(wo_ref[0, t], keeps[ui]))
        yc_scr[t] = acc

    perm_t = _class_perm(tr, cr, ncls, True)
    for i in range(ntile):
        yc = jnp.concatenate([yc_scr[k, i * cr:(i + 1) * cr, :] for k in range(ncls)], axis=0)
        hi = yc.astype(BF16)
        lo = (yc - hi.astype(F32)).astype(BF16)
        y_ref[i * tr:(i + 1) * tr, :] = _dot(perm_t, hi) + _dot(perm_t, lo)


def _ssm_call(h, kj, ws, wo, apow, s0, nbatch, ncls):
    m, d = h.shape
    t = m // nbatch
    nc = t // ncls
    assert nc & (nc - 1) == 0
    g, ns, _, n2 = apow.shape
    nslab = d // SLAB
    gps = g // nslab
    gc = SLAB // gps
    unit_lanes = (SLAB // n2) * gc
    bpb = max(b for b in range(1, nbatch + 1) if nbatch % b == 0 and b * nc <= S5_BLOCK_CHUNKS)
    nblk = nbatch // bpb
    r = bpb * nc
    cr = min(r, S5_PERM_CHUNKS)
    kern = functools.partial(_ssm_kernel, ncls=ncls, nb=bpb, nc=nc, half=n2 // 2, unit_lanes=unit_lanes, cr=cr)
    st = pl.BlockSpec((gps, 1, bpb, n2), lambda i, j: (i, j, 0, 0))
    y, sf = pl.pallas_call(
        kern,
        grid=(nslab, nblk),
        in_specs=[pl.BlockSpec((bpb * t, SLAB), lambda i, j: (j, i)),
                  pl.BlockSpec((1, ncls, gc, SLAB), lambda i, j: (i, 0, 0, 0)),
                  pl.BlockSpec((1, ncls, SLAB, n2), lambda i, j: (i, 0, 0, 0)),
                  pl.BlockSpec((1, ncls, SLAB, n2), lambda i, j: (i, 0, 0, 0)),
                  pl.BlockSpec((gps, ns, 2, n2), lambda i, j: (i, 0, 0, 0)),
                  st],
        out_specs=[pl.BlockSpec((bpb * t, SLAB), lambda i, j: (j, i)), st],
        out_shape=[jax.ShapeDtypeStruct((m, d), F32),
                   jax.ShapeDtypeStruct((g, nblk, bpb, n2), F32)],
        scratch_shapes=[pltpu.VMEM((ncls, r, SLAB), BF16), pltpu.VMEM((ncls, r, SLAB), F32),
                        pltpu.VMEM((SLAB // unit_lanes, r, SLAB), BF16), pltpu.VMEM((ncls, SLAB, SLAB), BF16)],
        compiler_params=_params("parallel", "parallel"),
        name="ssm",
    )(h, kj, ws, wo, apow, s0.reshape(g, nblk, bpb, n2))
    return y, sf.reshape(g, nbatch, n2)


def _ssm_tables(lam_re, lam_im, log_step, b_re, b_im, c_re, c_im):
    L = SSM_CHUNK
    g, n = lam_re.shape
    gc = b_re.shape[2]
    lr, li = lam_re.astype(F32), lam_im.astype(F32)
    dt = jnp.exp(log_step.astype(F32))[:, None]
    mag = jnp.exp(lr * dt)
    ar, ai = mag * jnp.cos(li * dt), mag * jnp.sin(li * dt)
    den = lr * lr + li * li
    xr = ar - 1.0
    f_re = (xr * lr + ai * li) / den
    f_im = (ai * lr - xr * li) / den
    b_ret, b_imt = jnp.swapaxes(b_re, 1, 2), jnp.swapaxes(b_im, 1, 2)
    bb_re = f_re[:, None, :] * b_ret - f_im[:, None, :] * b_imt
    bb_im = f_re[:, None, :] * b_imt + f_im[:, None, :] * b_ret
    j = jnp.arange(L + 1, dtype=F32)[:, None, None]
    pmag = jnp.exp(j * (lr * dt))
    pw_re, pw_im = pmag * jnp.cos(j * (li * dt)), pmag * jnp.sin(j * (li * dt))
    pb_re = pw_re[:, :, None, :] * bb_re - pw_im[:, :, None, :] * bb_im
    pb_im = pw_re[:, :, None, :] * bb_im + pw_im[:, :, None, :] * bb_re
    nslab = g * gc // SLAB
    n2 = 2 * n
    kj = (jnp.einsum('gcn,jgdn->jdgc', c_re, pb_re[:L], precision=HI)
          - jnp.einsum('gcn,jgdn->jdgc', c_im, pb_im[:L], precision=HI))
    kj = jnp.transpose(kj.reshape(L, gc, nslab, SLAB), (2, 0, 1, 3))

    def per_slab(w):
        return jnp.transpose(w.reshape(L, nslab, SLAB, n2), (1, 0, 2, 3))

    ws = per_slab(jnp.concatenate([pb_re[:L], pb_im[:L]], axis=3))
    cp_re = c_re[None] * pw_re[1:, :, None, :] - c_im[None] * pw_im[1:, :, None, :]
    cp_im = c_re[None] * pw_im[1:, :, None, :] + c_im[None] * pw_re[1:, :, None, :]
    wo = per_slab(jnp.concatenate([cp_re, -cp_im], axis=3))
    return kj.astype(BF16), ws.astype(BF16), wo.astype(BF16), pw_re, pw_im


def _ssm_state_powers(a_re, a_im, nc):
    steps = []
    for _ in range(max(nc.bit_length() - 1, 1)):
        steps.append(jnp.stack([jnp.concatenate([a_re, a_re], -1), jnp.concatenate([-a_im, a_im], -1)], axis=1))
        a_re, a_im = a_re * a_re - a_im * a_im, 2.0 * a_re * a_im
    return jnp.stack(steps, axis=1)


def _glu_kernel(x_ref, y_ref, nw_ref, sh_ref, sc_ref, d_ref, wv_ref, wg_ref, xr_ref, g_ref, o_ref, a_scr):
    @pl.when(pl.program_id(1) == 0)
    def _():
        for r0 in range(0, x_ref.shape[0], PROLOGUE_ROWS):
            rs = min(PROLOGUE_ROWS, x_ref.shape[0])
            h = _norm_mod(x_ref[r0:r0 + rs, :], nw_ref[...], _mod_rows(sh_ref, r0, rs), _mod_rows(sc_ref, r0, rs))
            a_scr[r0:r0 + rs, :] = jax.nn.gelu(y_ref[r0:r0 + rs, :] + d_ref[...] * h).astype(BF16)

    a = a_scr[...]
    val = _dot(a, wv_ref[...])
    gate = _dot(a, wg_ref[...])
    o_ref[...] = xr_ref[...] + g_ref[0] * (val / (1.0 + jnp.exp(-gate)))


def _glu_call(x, y, nw, sh, sc, dskip, w, g1, nb):
    m, d = x.shape
    rows = _Rows(m, nb, 512)
    tm, r = rows.tm, sh.shape[1]
    tn = _tile(d, 512)
    nn = d // tn
    return pl.pallas_call(
        _glu_kernel,
        grid=(rows.nt, nn),
        in_specs=[pl.BlockSpec((tm, d), lambda i, j: (i, 0)),
                  pl.BlockSpec((tm, d), lambda i, j: (i, 0)),
                  pl.BlockSpec((1, d), lambda i, j: (0, 0)),
                  rows.mod_spec(r, d), rows.mod_spec(r, d),
                  pl.BlockSpec((1, d), lambda i, j: (0, 0)),
                  pl.BlockSpec((d, tn), lambda i, j: (0, j)),
                  pl.BlockSpec((d, tn), lambda i, j: (0, j + nn)),
                  pl.BlockSpec((tm, tn), lambda i, j: (i, j)),
                  rows.mod_spec(r, tn, col=True)],
        out_specs=pl.BlockSpec((tm, tn), lambda i, j: (i, j)),
        out_shape=jax.ShapeDtypeStruct((m, d), F32),
        scratch_shapes=[pltpu.VMEM((tm, d), BF16)],
        compiler_params=_params("parallel", "arbitrary"),
        name="glu",
    )(x, y, nw, sh, sc, dskip, w, w, x, g1)


def _mlp_kernel(x_ref, nw_ref, sh_ref, sc_ref, g_ref, up_ref, dn_ref, o_ref, h_scr, *, nf):
    f = pl.program_id(1)

    @pl.when(f == 0)
    def _():
        for r0 in range(0, x_ref.shape[0], PROLOGUE_ROWS):
            rs = min(PROLOGUE_ROWS, x_ref.shape[0])
            h = _norm_mod(x_ref[r0:r0 + rs, :], nw_ref[...], _mod_rows(sh_ref, r0, rs), _mod_rows(sc_ref, r0, rs))
            h_scr[r0:r0 + rs, :] = h.astype(BF16)
        o_ref[...] = jnp.zeros(o_ref.shape, F32)

    a = jnp.maximum(_dot(h_scr[...], up_ref[...]), 0.0)
    o_ref[...] += _dot((a * a).astype(BF16), dn_ref[...])

    @pl.when(f == nf - 1)
    def _():
        o_ref[...] = x_ref[...] + g_ref[0] * o_ref[...]


def _mlp_call(x, nw, sh, sc, g2, up, dn, nb):
    m, d = x.shape
    ff = up.shape[1]
    rows = _Rows(m, nb, 512)
    tm, r = rows.tm, sh.shape[1]
    tf = _tile(ff, 1024)
    nf = ff // tf
    return pl.pallas_call(
        functools.partial(_mlp_kernel, nf=nf),
        grid=(rows.nt, nf),
        in_specs=[pl.BlockSpec((tm, d), lambda i, j: (i, 0)),
                  pl.BlockSpec((1, d), lambda i, j: (0, 0)),
                  rows.mod_spec(r, d), rows.mod_spec(r, d), rows.mod_spec(r, d),
                  pl.BlockSpec((d, tf), lambda i, j: (0, j)),
                  pl.BlockSpec((tf, d), lambda i, j: (j, 0))],
        out_specs=pl.BlockSpec((tm, d), lambda i, j: (i, 0)),
        out_shape=jax.ShapeDtypeStruct((m, d), F32),
        scratch_shapes=[pltpu.VMEM((tm, d), BF16)],
        compiler_params=_params("parallel", "arbitrary"),
        name="mlp",
    )(x, nw, sh, sc, g2, up, dn)


def _mod_rows(ref, r0, ts):
    return ref[0] if ref.shape[1] == 1 else ref[0, r0:r0 + ts, :]


def _qproj_kernel(x_ref, nw_ref, sh_ref, sc_ref, w_ref, cw_ref, q_ref, *, hd, ts, qscale):
    cw = cw_ref[...] * qscale
    for r0 in range(0, x_ref.shape[0], ts):
        h = _norm_mod(x_ref[r0:r0 + ts, :], nw_ref[...], _mod_rows(sh_ref, r0, ts), _mod_rows(sc_ref, r0, ts))
        q = _dot(h.astype(BF16), w_ref[...])
        for c in range(q.shape[1] // hd):
            q_ref[r0:r0 + ts, c * hd:(c + 1) * hd] = _rms(q[:, c * hd:(c + 1) * hd], cw).astype(BF16)


def _qproj_call(x, nw, sh, sc, wq, qnw, nb, qscale):
    m, d = x.shape
    n = wq.shape[1]
    hd = qnw.shape[1]
    rows = _Rows(m, nb, 512)
    tm, r = rows.tm, sh.shape[1]
    tpn = rows.tiles_per_nb
    mod = pl.BlockSpec((1, r, d), lambda i: (i // tpn, 0, 0))
    return pl.pallas_call(
        functools.partial(_qproj_kernel, hd=hd, ts=min(tm, 256), qscale=qscale),
        grid=(rows.nt,),
        in_specs=[pl.BlockSpec((tm, d), lambda i: (i, 0)),
                  pl.BlockSpec((1, d), lambda i: (0, 0)),
                  mod, mod,
                  pl.BlockSpec((d, n), lambda i: (0, 0)),
                  pl.BlockSpec((1, hd), lambda i: (0, 0))],
        out_specs=pl.BlockSpec((tm, n), lambda i: (i, 0)),
        out_shape=jax.ShapeDtypeStruct((m, n), BF16),
        compiler_params=_params("parallel"),
        name="q_proj",
    )(x, nw, sh, sc, wq, qnw)


def _kvproj_kernel(x_ref, nw_ref, sh_ref, sc_ref, wk_ref, wv_ref, cw_ref, k_ref, v_ref, kb_ref, vb_ref, h_scr,
                   *, hd):
    @pl.when(pl.program_id(1) == 0)
    def _():
        for r0 in range(0, x_ref.shape[0], PROLOGUE_ROWS):
            rs = min(PROLOGUE_ROWS, x_ref.shape[0])
            h = _norm_mod(x_ref[r0:r0 + rs, :], nw_ref[...], _mod_rows(sh_ref, r0, rs), _mod_rows(sc_ref, r0, rs))
            h_scr[r0:r0 + rs, :] = h.astype(BF16)

    h = h_scr[...]
    k = _chunk_rms(_dot(h, wk_ref[...]), cw_ref[...], hd)
    v = _dot(h, wv_ref[...])
    k_ref[...] = k
    v_ref[...] = v
    kb_ref[...] = k.astype(BF16)
    vb_ref[...] = v.astype(BF16)


def _kvproj_call(x, nw, sh, sc, w, knw, nb):
    m, d = x.shape
    nk = w.shape[1] // 2
    hd = knw.shape[1]
    rows = _Rows(m, nb, 512)
    tm, r = rows.tm, sh.shape[1]
    tn = _tile(nk, 512)
    nn = nk // tn
    out_spec = pl.BlockSpec((tm, tn), lambda i, j: (i, j))
    return pl.pallas_call(
        functools.partial(_kvproj_kernel, hd=hd),
        grid=(rows.nt, nn),
        in_specs=[pl.BlockSpec((tm, d), lambda i, j: (i, 0)),
                  pl.BlockSpec((1, d), lambda i, j: (0, 0)),
                  rows.mod_spec(r, d), rows.mod_spec(r, d),
                  pl.BlockSpec((d, tn), lambda i, j: (0, j)),
                  pl.BlockSpec((d, tn), lambda i, j: (0, j + nn)),
                  pl.BlockSpec((1, hd), lambda i, j: (0, 0))],
        out_specs=[out_spec, out_spec, out_spec, out_spec],
        out_shape=[jax.ShapeDtypeStruct((m, nk), F32), jax.ShapeDtypeStruct((m, nk), F32),
                   jax.ShapeDtypeStruct((m, nk), BF16), jax.ShapeDtypeStruct((m, nk), BF16)],
        scratch_shapes=[pltpu.VMEM((tm, d), BF16)],
        compiler_params=_params("parallel", "arbitrary"),
        name="kv_proj",
    )(x, nw, sh, sc, w, w, knw)


def _oproj_kernel(o_ref, w_ref, x_ref, g_ref, y_ref):
    y_ref[...] = x_ref[...] + g_ref[0] * _dot(o_ref[...], w_ref[...])


def _oproj_call(o, wo, x, g1, nb):
    m, d = x.shape
    rows = _Rows(m, nb, 512)
    tm, r = rows.tm, g1.shape[1]
    tpn = rows.tiles_per_nb
    return pl.pallas_call(
        _oproj_kernel,
        grid=(rows.nt,),
        in_specs=[pl.BlockSpec((tm, o.shape[1]), lambda i: (i, 0)),
                  pl.BlockSpec((o.shape[1], d), lambda i: (0, 0)),
                  pl.BlockSpec((tm, d), lambda i: (i, 0)),
                  pl.BlockSpec((1, r, d), lambda i: (i // tpn, 0, 0))],
        out_specs=pl.BlockSpec((tm, d), lambda i: (i, 0)),
        out_shape=jax.ShapeDtypeStruct((m, d), F32),
        compiler_params=_params("parallel"),
        name="o_proj",
    )(o, wo, x, g1)


def _lambda(lq1, lk1, lq2, lk2, lam_init):
    return (jnp.exp(jnp.sum(lq1 * lk1, axis=-1, keepdims=True))
            - jnp.exp(jnp.sum(lq2 * lk2, axis=-1, keepdims=True)) + lam_init)


def _pattn_kernel(q_ref, k_ref, v_ref, lq1_ref, lk1_ref, lq2_ref, lk2_ref, swc_ref, o_ref, m_scr, l_scr, acc_scr,
                  *, hd, tq, lam_init):
    qi = pl.program_id(2)
    qt = q_ref[...].astype(F32).T
    top = lax.broadcasted_iota(jnp.int32, qt.shape, 0) < hd
    qw = jnp.concatenate([jnp.where(top, qt, 0.0), jnp.where(top, 0.0, qt)], axis=1).astype(BF16)
    m_scr[...] = jnp.full(m_scr.shape, -jnp.inf, F32)
    l_scr[...] = jnp.zeros(l_scr.shape, F32)
    acc_scr[...] = jnp.zeros(acc_scr.shape, F32)

    def block(kstart, masked):
        k = k_ref[pl.ds(kstart, tq), :]
        v = v_ref[pl.ds(kstart, tq), :]
        s = _dot(k, qw)
        if masked:
            key = lax.broadcasted_iota(jnp.int32, s.shape, 0)
            qry = _irem(lax.broadcasted_iota(jnp.int32, s.shape, 1), tq)
            s = jnp.where(key <= qry, s, -jnp.inf)
        m_prev = m_scr[...]
        m_new = jnp.maximum(m_prev, jnp.max(s, axis=0, keepdims=True))
        alpha = jnp.exp2(m_prev - m_new)
        p = jnp.exp2(s - m_new)
        l_scr[...] = alpha * l_scr[...] + jnp.sum(p, axis=0, keepdims=True)
        pv = lax.dot_general(v, p.astype(BF16), (((0,), (0,)), ((), ())), preferred_element_type=F32)
        acc_scr[...] = alpha * acc_scr[...] + pv
        m_scr[...] = m_new

    def body(j, carry):
        block(pl.multiple_of(j * tq, tq), False)
        return carry

    lax.fori_loop(0, qi, body, 0)
    block(pl.multiple_of(qi * tq, tq), True)
    lam = _lambda(lq1_ref[...], lk1_ref[...], lq2_ref[...], lk2_ref[...], lam_init)
    on = acc_scr[...] / l_scr[...]
    o = on[:, :tq] - lam * on[:, tq:]
    ms = jnp.mean(o * o, axis=0, keepdims=True)
    o = o * lax.rsqrt(ms + NORM_EPS) * swc_ref[...] * (1.0 - lam_init)
    o_ref[...] = o.T.astype(BF16)


def _pattn_call(q, kb, vb, lq1, lk1, lq2, lk2, subw, nbatch, lam_init):
    m, d = q.shape
    vd = subw.shape[1]
    hd = lq1.shape[1]
    nh = d // vd
    t = m // nbatch
    tq = _tile(t, ATTN_TILE)
    nq = t // tq
    kern = functools.partial(_pattn_kernel, hd=hd, tq=tq, lam_init=lam_init)
    vec = pl.BlockSpec((1, hd), lambda b, h, i: (0, 0))
    return pl.pallas_call(
        kern,
        grid=(nbatch, nh, nq),
        in_specs=[pl.BlockSpec((tq, vd), lambda b, h, i: (b * nq + i, h)),
                  pl.BlockSpec((t, vd), lambda b, h, i: (b, h)),
                  pl.BlockSpec((t, vd), lambda b, h, i: (b, h)),
                  vec, vec, vec, vec,
                  pl.BlockSpec((vd, 1), lambda b, h, i: (0, 0))],
        out_specs=pl.BlockSpec((tq, vd), lambda b, h, i: (b * nq + i, h)),
        out_shape=jax.ShapeDtypeStruct((m, d), BF16),
        scratch_shapes=[pltpu.VMEM((1, 2 * tq), F32), pltpu.VMEM((1, 2 * tq), F32), pltpu.VMEM((vd, 2 * tq), F32)],
        compiler_params=_params("parallel", "parallel", "parallel"),
        name="prompt_attn",
    )(q, kb, vb, lq1, lk1, lq2, lk2, subw.reshape(vd, 1))


def _idiv(x, n):
    if n & (n - 1) == 0:
        return lax.shift_right_logical(x, jnp.full_like(x, n.bit_length() - 1))
    return lax.div(x, jnp.full_like(x, n))


def _irem(x, n):
    if n & (n - 1) == 0:
        return x & (n - 1)
    return lax.rem(x, jnp.full_like(x, n))


def _sattn_kernel(pt_ref, qw_ref, *refs, npg, nh, nq, ncol, lam_init):
    kp_refs = refs[:npg]
    vp_refs = refs[npg:2 * npg]
    kn_ref, vn_ref, lq1_ref, lk1_ref, lq2_ref, lk2_ref, swc_ref, o_ref, m_scr, l_scr, acc_scr = refs[2 * npg:]
    s_id = pl.program_id(1)
    qw = qw_ref[0]
    jp = qw.shape[1]

    @pl.when(s_id == 0)
    def _():
        m_scr[...] = jnp.full(m_scr.shape, -jnp.inf, F32)
        l_scr[...] = jnp.zeros(l_scr.shape, F32)
        acc_scr[...] = jnp.zeros(acc_scr.shape, F32)

    def same_head(nrows):
        row = lax.broadcasted_iota(jnp.int32, (nrows, jp), 0)
        col = lax.broadcasted_iota(jnp.int32, (nrows, jp), 1)
        return row, col, (_irem(row, nh) == _idiv(col, 2 * nq)) | (col >= ncol)

    def update(k, v, keep):
        s = jnp.where(keep, _dot(k, qw), -jnp.inf)
        m_prev = m_scr[...]
        m_new = jnp.maximum(m_prev, jnp.max(s, axis=0, keepdims=True))
        alpha = jnp.exp2(m_prev - m_new)
        p = jnp.exp2(s - m_new)
        l_scr[...] = alpha * l_scr[...] + jnp.sum(p, axis=0, keepdims=True)
        pv = lax.dot_general(v, p.astype(BF16), (((0,), (0,)), ((), ())), preferred_element_type=F32)
        acc_scr[...] = alpha * acc_scr[...] + pv
        m_scr[...] = m_new

    _, _, keep_page = same_head(kp_refs[0].shape[1])
    for j in range(npg):
        update(kp_refs[j][0].astype(BF16), vp_refs[j][0].astype(BF16), keep_page)

    @pl.when(s_id == pl.num_programs(1) - 1)
    def _():
        row, col, keep = same_head(kn_ref.shape[1])
        update(kn_ref[0], vn_ref[0], keep & ((_idiv(row, nh) <= _irem(col, nq)) | (col >= ncol)))
        lam = _lambda(lq1_ref[...], lk1_ref[...], lq2_ref[...], lk2_ref[...], lam_init)
        on = acc_scr[...] / l_scr[...]
        o = on - lam * pltpu.roll(on, jp - nq, axis=1)
        ms = jnp.mean(o * o, axis=0, keepdims=True)
        o_ref[0] = o * lax.rsqrt(ms + NORM_EPS) * swc_ref[...] * (1.0 - lam_init)


def _sattn_call(page_table, qw, cache_k, cache_v, k_new, v_new, lq1, lk1, lq2, lk2, subw_col, nh, nq, ncol,
                lam_init):
    nbatch, kd, jp = qw.shape
    prow = cache_k.shape[1]
    vd = cache_v.shape[2]
    hd = lq1.shape[1]
    npages = page_table.shape[1]
    npg = _tile(npages, 8)
    nk = k_new.shape[1]
    kern = functools.partial(_sattn_kernel, npg=npg, nh=nh, nq=nq, ncol=ncol, lam_init=lam_init)

    def page_spec(j, width):
        return pl.BlockSpec((1, prow, width), lambda b, s, pt: (pt[b, s * npg + j], 0, 0))

    vec = pl.BlockSpec((1, hd), lambda b, s, pt: (0, 0))
    grid_spec = pltpu.PrefetchScalarGridSpec(
        num_scalar_prefetch=1,
        grid=(nbatch, npages // npg),
        in_specs=([pl.BlockSpec((1, kd, jp), lambda b, s, pt: (b, 0, 0))]
                  + [page_spec(j, kd) for j in range(npg)] + [page_spec(j, vd) for j in range(npg)]
                  + [pl.BlockSpec((1, nk, kd), lambda b, s, pt: (b, 0, 0)),
                     pl.BlockSpec((1, nk, vd), lambda b, s, pt: (b, 0, 0)),
                     vec, vec, vec, vec,
                     pl.BlockSpec((vd, 1), lambda b, s, pt: (0, 0))]),
        out_specs=pl.BlockSpec((1, vd, jp), lambda b, s, pt: (b, 0, 0)),
        scratch_shapes=[pltpu.VMEM((1, jp), F32), pltpu.VMEM((1, jp), F32), pltpu.VMEM((vd, jp), F32)],
    )
    return pl.pallas_call(
        kern,
        grid_spec=grid_spec,
        out_shape=jax.ShapeDtypeStruct((nbatch, vd, jp), F32),
        compiler_params=_params("parallel", "arbitrary"),
        name="sample_attn",
    )(page_table, qw, *([cache_k] * npg), *([cache_v] * npg), k_new, v_new, lq1, lk1, lq2, lk2, subw_col)


def _run_group(x3, mods, kvmod, s_re0, s_im0, attend, p, nb):
    nbatch, t, d = x3.shape
    m = nbatch * t
    x = x3.reshape(m, d)
    depth = p['norm_w'].shape[0]
    n_a = p['lam_re'].shape[0]
    g, n = p['lam_re'].shape[1:]
    tv = min(SSM_CHUNK, t)
    assert t % tv == 0
    nc = t // tv
    new_re, new_im = [], []
    k = v = kb = vb = None
    for l in range(depth):
        sh1, sc1, g1, sh2, sc2, g2 = mods[l]
        nw1 = p['norm_w'][l, 0][None]
        if l < n_a:
            kj, ws, wo, pw_re, pw_im = p['ssm_tabs'][l]
            apow = _ssm_state_powers(pw_re[tv], pw_im[tv], nc)
            h = _norm_mod_call(x, nw1, sh1, sc1, nb)
            s0 = jnp.concatenate([jnp.transpose(s_re0[l], (1, 0, 2)), jnp.transpose(s_im0[l], (1, 0, 2))], -1)
            y, sf = _ssm_call(h, kj, ws, wo, apow, s0.astype(F32), nbatch, tv)
            new_re.append(jnp.transpose(sf[..., :n], (1, 0, 2)))
            new_im.append(jnp.transpose(sf[..., n:], (1, 0, 2)))
            x = _glu_call(x, y, nw1, sh1, sc1, p['ssm_d'][l][None], p['glu_w'][l], g1, nb)
        else:
            j = l - n_a
            lam_init = 0.8 - 0.6 * math.exp(-0.3 * l)
            hd = p['q_norm_w'].shape[1]
            q = _qproj_call(x, nw1, sh1, sc1, p['attn_wq'][j], p['q_norm_w'][j][None], nb,
                            hd ** -0.5 * math.log2(math.e))
            o = attend(q, kb, vb, p['lambda_q1'][j][None], p['lambda_k1'][j][None], p['lambda_q2'][j][None],
                       p['lambda_k2'][j][None], p['subln_w'][j][None], lam_init)
            x = _oproj_call(o, p['attn_wo'][j], x, g1, nb)
        x = _mlp_call(x, p['norm_w'][l, 1][None], sh2, sc2, g2, p['mlp_up'][l], p['mlp_down'][l], nb)
        if l == n_a - 1:
            k, v, kb, vb = _kvproj_call(x, p['kv_norm_w'][None], kvmod[0], kvmod[1], p['kv_w'], p['k_norm_w'][None],
                                        nb)
    return x.reshape(nbatch, t, d), jnp.stack(new_re), jnp.stack(new_im), k, v


def kernel(x_prompt, x_sample, state_ssm_re, state_ssm_im, cache_k, cache_v, page_table, c_prompt, c_sample, ada_w, ada_b, norm_w, mlp_up, mlp_down, ssm_lambda_re, ssm_lambda_im, ssm_log_step, ssm_b_re, ssm_b_im, ssm_c_re, ssm_c_im, ssm_d, glu_w, kv_ada_w, kv_ada_b, kv_norm_w, kv_w, k_norm_w, attn_wq, q_norm_w, lambda_q1, lambda_k1, lambda_q2, lambda_k2, subln_w, attn_wo):
    bp, tp, d = x_prompt.shape
    bs, ts, _ = x_sample.shape
    depth = ada_w.shape[0]
    n_a = ssm_lambda_re.shape[0]
    nh, vd = cache_v.shape[2], cache_v.shape[3]
    hd = q_norm_w.shape[1]

    p = {
        'norm_w': norm_w, 'mlp_up': mlp_up.astype(BF16), 'mlp_down': mlp_down.astype(BF16),
        'lam_re': ssm_lambda_re,
        'ssm_tabs': [_ssm_tables(ssm_lambda_re[l], ssm_lambda_im[l], ssm_log_step[l], ssm_b_re[l].astype(F32),
                                 ssm_b_im[l].astype(F32), ssm_c_re[l].astype(F32), ssm_c_im[l].astype(F32))
                     for l in range(n_a)],
        'ssm_d': ssm_d, 'glu_w': glu_w.astype(BF16), 'kv_norm_w': kv_norm_w, 'kv_w': kv_w.astype(BF16),
        'k_norm_w': k_norm_w, 'attn_wq': attn_wq.astype(BF16), 'q_norm_w': q_norm_w,
        'lambda_q1': lambda_q1, 'lambda_k1': lambda_k1, 'lambda_q2': lambda_q2, 'lambda_k2': lambda_k2,
        'subln_w': subln_w, 'attn_wo': attn_wo.astype(BF16),
    }

    nc_rows = bp + bs
    c_all = jnp.concatenate([c_prompt, c_sample], axis=0)
    c_all = jnp.pad(c_all, ((0, -nc_rows % 16), (0, 0))).astype(BF16)
    mod_all = _ada(c_all, ada_w, ada_b[:, None, :])
    kvmod_all = _ada(c_all, kv_ada_w[None], kv_ada_b[None, None, :])[0]

    def split_mods(lo, hi, rep):
        def shape(a):
            return a[:, None, :] if rep == 1 else jnp.repeat(a, rep, axis=0)[None]
        mods = [[shape(mod_all[l, lo:hi, i * d:(i + 1) * d]) for i in range(6)] for l in range(depth)]
        kvmod = [shape(kvmod_all[lo:hi, i * d:(i + 1) * d]) for i in range(2)]
        return mods, kvmod

    mods_p, kvmod_p = split_mods(0, bp, 1)
    zeros = jnp.zeros((n_a, bp) + ssm_lambda_re.shape[1:], F32)

    def attend_prompt(q, kb, vb, lq1, lk1, lq2, lk2, subw, lam_init):
        return _pattn_call(q, kb, vb, lq1, lk1, lq2, lk2, subw, bp, lam_init)

    y_p, re_p, im_p, k_p, v_p = _run_group(x_prompt, mods_p, kvmod_p, zeros, zeros, attend_prompt, p, bp)

    mods_s, kvmod_s = split_mods(bp, bp + bs, ts)
    page = cache_k.shape[1]
    ck = cache_k.reshape(cache_k.shape[0], page * nh, 2 * hd)
    cv = cache_v.reshape(cache_v.shape[0], page * nh, vd)
    eye2 = jnp.eye(2, dtype=BF16)
    ncol = nh * 2 * ts
    jp = -(-ncol // 128) * 128

    def attend_sample(q, kb, vb, lq1, lk1, lq2, lk2, subw, lam_init):
        q5 = q.reshape(bs, ts, nh, 2, hd)
        qw = jnp.einsum('bihmd,mM->bmdhMi', q5, eye2).reshape(bs, 2 * hd, ncol)
        qw = jnp.pad(qw, ((0, 0), (0, 0), (0, jp - ncol)))
        k_new = kb.reshape(bs, ts * nh, 2 * hd)
        v_new = vb.reshape(bs, ts * nh, vd)
        ot = _sattn_call(page_table, qw, ck, cv, k_new, v_new, lq1, lk1, lq2, lk2, subw.reshape(vd, 1),
                         nh, ts, ncol, lam_init)
        o = ot[:, :, :ncol].reshape(bs, vd, nh, 2, ts)[:, :, :, 0, :]
        return jnp.transpose(o, (0, 3, 2, 1)).reshape(bs * ts, d).astype(BF16)

    y_s, re_s, im_s, k_s, v_s = _run_group(x_sample, mods_s, kvmod_s, state_ssm_re, state_ssm_im, attend_sample,
                                           p, 1)

    return (y_p, y_s, re_p, im_p,
            k_p.reshape(bp, tp, nh, 2 * hd), v_p.reshape(bp, tp, nh, vd),
            re_s, im_s,
            k_s.reshape(bs, ts, nh, 2 * hd), v_s.reshape(bs, ts, nh, vd))
```
